```python
import jax
import jax.numpy as jnp
from jax import lax
import numpy as np

D_MODEL = 2048
BATCH = 4
SEQ = 2048
DEPTH = 4
DEC_BATCH = 32
DEC_SEQ = 1
PAST_LEN = 16384
PAGE_SIZE = 128

NSA_HEADS = 16
NSA_KV = 4
NSA_HD = D_MODEL // NSA_HEADS
NSA_GROUP = NSA_HEADS // NSA_KV
NSA_ROT = NSA_HD // 4
NSA_Q = NSA_HEADS * NSA_HD
NSA_KVD = NSA_KV * NSA_HD
NSA_IN = NSA_Q + 6 * NSA_KVD + 3 * NSA_HEADS
CMP_LEN = 32
CMP_STRIDE = 16
CMP_HID = 2 * NSA_HD
SLC_LEN = 64
SLC_TOPK = 16
SLC_QCHUNK = 32
NSA_WINDOW = 512
FORCE_SCORE = 1e4
SWA_HEADS = 32
SWA_KV = 4
SWA_HD = D_MODEL // SWA_HEADS
SWA_GROUP = SWA_HEADS // SWA_KV
SWA_ROT = SWA_HD // 4
SWA_Q = SWA_HEADS * SWA_HD
SWA_KVD = SWA_KV * SWA_HD
SWA_IN = SWA_Q + 2 * SWA_KVD
SWA_WINDOW = 128
BAND_BLOCK = 128
ROPE_THETA = 500000.0
D_FF = ((8 * D_MODEL // 3 + 255) // 256) * 256
CONV_W = 3
N_NSA = (DEPTH + 1) // 2
N_SWA = DEPTH // 2
ALPHA = (2 * DEPTH) ** 0.25
BETA = (8 * DEPTH) ** -0.25
LN_EPS = 1e-5
NEG = -1e30

kernel_name = 'nsa_swa_sink_convffn_deepnorm_step'


def _layernorm(x, g, b):
    xf = x.astype(jnp.float32)
    mu = xf.mean(-1, keepdims=True)
    var = jnp.square(xf - mu).mean(-1, keepdims=True)
    return ((xf - mu) * lax.rsqrt(var + LN_EPS) * g + b).astype(x.dtype)


def _rope(x, pos, rot):
    half = rot // 2
    inv = ROPE_THETA ** (-jnp.arange(half, dtype=jnp.float32) / half)
    ang = pos.astype(jnp.float32)[:, None] * inv[None, :]
    cos, sin = jnp.cos(ang)[:, None, :], jnp.sin(ang)[:, None, :]
    xf = x[..., :rot].astype(jnp.float32)
    x1, x2 = xf[..., :half], xf[..., half:]
    xr = jnp.concatenate([x1 * cos - x2 * sin, x2 * cos + x1 * sin], -1).astype(x.dtype)
    return jnp.concatenate([xr, x[..., rot:]], -1)


def _masked_softmax(s, mask, sink=None):
    s = jnp.where(mask, s.astype(jnp.float32), NEG)
    if sink is None:
        p = jax.nn.softmax(s, axis=-1)
    else:
        m = jnp.maximum(s.max(-1, keepdims=True), sink)
        e = jnp.exp(s - m)
        p = e / (e.sum(-1, keepdims=True) + jnp.exp(sink - m))
    return p * mask


def _band_attend(q, k, v, qpos, kpos, window, sink=None):
    s = jnp.einsum('bnqgrd,bnkgd->bngrqk', q, k) * (q.shape[-1] ** -0.5)
    dist = qpos[:, :, None] - kpos[:, None, :]
    mask = (dist >= 0) & (dist < window) & (kpos[:, None, :] >= 0)
    p = _masked_softmax(s, mask[None, :, None, None], sink)
    return jnp.einsum('bngrqk,bnkgd->bnqgrd', p.astype(v.dtype), v)


def _band_prompt(q, k, v, window, sink=None):
    B, S = q.shape[:2]
    nb, npre = S // BAND_BLOCK, window // BAND_BLOCK
    pad = ((0, 0), (npre * BAND_BLOCK, 0), (0, 0), (0, 0))
    kb = jnp.pad(k, pad).reshape(B, nb + npre, BAND_BLOCK, *k.shape[2:])
    vb = jnp.pad(v, pad).reshape(B, nb + npre, BAND_BLOCK, *v.shape[2:])
    kband = jnp.concatenate([kb[:, j:j + nb] for j in range(npre + 1)], axis=2)
    vband = jnp.concatenate([vb[:, j:j + nb] for j in range(npre + 1)], axis=2)
    start = jnp.arange(nb)[:, None] * BAND_BLOCK
    qpos = start + jnp.arange(BAND_BLOCK)[None, :]
    kpos = start - npre * BAND_BLOCK + jnp.arange((npre + 1) * BAND_BLOCK)[None, :]
    qb = q.reshape(B, nb, BAND_BLOCK, *q.shape[2:])
    return _band_attend(qb, kband, vband, qpos, kpos, window, sink).reshape(q.shape)


def _band_decode(q, k_new, v_new, buf_k, buf_v, window, sink=None):
    T, Lb = q.shape[1], buf_k.shape[1]
    kc = jnp.concatenate([buf_k, k_new], 1)
    vc = jnp.concatenate([buf_v, v_new], 1)
    qpos = (PAST_LEN + jnp.arange(T))[None, :]
    kpos = (PAST_LEN - Lb + jnp.arange(Lb + T))[None, :]
    o = _band_attend(q[:, None], kc[:, None], vc[:, None], qpos, kpos, window, sink)[:, 0]
    return o, kc[:, T:], vc[:, T:]


def _cmp_chunk_proj(k, w1):
    B, L = k.shape[:2]
    kc = k.reshape(B, L // CMP_STRIDE, CMP_STRIDE, *k.shape[2:])
    w1r = w1.reshape(CMP_LEN // CMP_STRIDE, CMP_STRIDE, *w1.shape[1:])
    return [jnp.einsum('bnsgd,sdh->bngh', kc, w1r[c]) for c in range(CMP_LEN // CMP_STRIDE)]


def _cmp_combine(projs, pe, w1, b1, w2):
    n = projs[0].shape[1] - len(projs) + 1
    h = jnp.einsum('ld,ldh->h', pe, w1) + b1
    for c, pc in enumerate(projs):
        h = h + pc[:, c:c + n]
    return jnp.einsum('bngh,he->bnge', jax.nn.gelu(h), w2)


def _overlap(n_cmp, n_slc):
    i = jnp.arange(n_cmp)[:, None] * CMP_STRIDE
    j = jnp.arange(n_slc)[None, :] * SLC_LEN
    return ((i < j + SLC_LEN) & (i + CMP_LEN > j)).astype(jnp.float32)


def _nsa_compressed(q, k_blk, v_blk, qpos, n_slc):
    n = k_blk.shape[1]
    s = jnp.einsum('btgrd,bngd->btgrn', q, k_blk) * (NSA_HD ** -0.5)
    cend = jnp.arange(n) * CMP_STRIDE + CMP_LEN - 1
    mask = (cend[None, :] <= qpos[:, None])[None, :, None, None, :]
    p = _masked_softmax(s, mask)
    o = jnp.einsum('btgrn,bngd->btgrd', p.astype(v_blk.dtype), v_blk)
    imp = jnp.einsum('btgrn,nj->btgj', p, _overlap(n, n_slc))
    j = jnp.arange(n_slc)[None, :]
    cur = (qpos // SLC_LEN)[:, None]
    valid = j <= cur
    forced = (j == 0) | (j == cur) | (j == cur - 1)
    imp = jnp.where(forced[None, :, None], FORCE_SCORE, jnp.where(valid[None, :, None], imp, -1.0))
    _, idx = lax.top_k(imp, min(SLC_TOPK, n_slc))
    return o, idx


def _slc_attend(q, kg, vg, kpos, qpos):
    B, T, G, K, Ls, hd = kg.shape
    kg = kg.reshape(B, T, G, K * Ls, hd)
    vg = vg.reshape(B, T, G, K * Ls, hd)
    s = jnp.einsum('btgrd,btgkd->btgrk', q, kg) * (hd ** -0.5)
    mask = (kpos.reshape(B, T, G, K * Ls) <= qpos[None, :, None, None])[:, :, :, None]
    p = _masked_softmax(s, mask)
    return jnp.einsum('btgrk,btgkd->btgrd', p.astype(vg.dtype), vg)


def _slc_prompt(q, k, v, idx):
    B, S, G, R, hd = q.shape
    kb = k.reshape(B, S // SLC_LEN, SLC_LEN, G, hd).transpose(0, 3, 1, 2, 4)
    vb = v.reshape(B, S // SLC_LEN, SLC_LEN, G, hd).transpose(0, 3, 1, 2, 4)
    bi = jnp.arange(B)[:, None, None, None]
    gi = jnp.arange(G)[None, None, :, None]
    nc = S // SLC_QCHUNK

    def chunk(args):
        qc, ic, pc = args
        kpos = ic[..., None] * SLC_LEN + jnp.arange(SLC_LEN)
        return _slc_attend(qc, kb[bi, gi, ic], vb[bi, gi, ic], kpos, pc)

    qs = q.reshape(B, nc, SLC_QCHUNK, G, R, hd).swapaxes(0, 1)
    ids = idx.reshape(B, nc, SLC_QCHUNK, G, idx.shape[-1]).swapaxes(0, 1)
    ps = jnp.arange(S).reshape(nc, SLC_QCHUNK)
    o = lax.map(chunk, (qs, ids, ps))
    return o.swapaxes(0, 1).reshape(B, S, G, R, hd)


def _slc_decode(q, k_new, v_new, pool_k, pool_v, layer, page_table, idx, qpos):
    B, T = q.shape[:2]
    n_past = PAST_LEN // SLC_LEN
    n_new = -(-T // SLC_LEN)
    bpp = PAGE_SIZE // SLC_LEN
    bi = jnp.arange(B)[:, None, None, None]
    gi = jnp.arange(NSA_KV)[None, None, :, None]
    off = jnp.arange(SLC_LEN)
    jp = jnp.minimum(idx, n_past - 1)
    phys = page_table[bi, jp // bpp][..., None]
    row = (jp % bpp)[..., None] * SLC_LEN + off
    g5 = gi[..., None]
    pad = n_new * SLC_LEN - T
    jn = jnp.clip(idx - n_past, 0, n_new - 1)

    def new_blocks(new):
        nb = jnp.pad(new, ((0, 0), (0, pad), (0, 0), (0, 0)))
        nb = nb.reshape(B, n_new, SLC_LEN, NSA_KV, NSA_HD).transpose(0, 3, 1, 2, 4)
        return nb[bi, gi, jn]

    is_past = (idx < n_past)[..., None, None]
    kg = jnp.where(is_past, pool_k[layer, phys, row, g5], new_blocks(k_new))
    vg = jnp.where(is_past, pool_v[layer, phys, row, g5], new_blocks(v_new))
    kpos = idx[..., None] * SLC_LEN + off
    return _slc_attend(q, kg, vg, kpos, qpos)


def _nsa_project(x, pos, w_in):
    B, T, _ = x.shape
    h = x @ w_in
    q = h[..., :NSA_Q].reshape(B, T, NSA_HEADS, NSA_HD)
    kc, vc, ks, vs, kw, vw = [h[..., NSA_Q + i * NSA_KVD:NSA_Q + (i + 1) * NSA_KVD].reshape(B, T, NSA_KV, NSA_HD)
                              for i in range(6)]
    gates = jax.nn.sigmoid(h[..., NSA_Q + 6 * NSA_KVD:].astype(jnp.float32)).reshape(B, T, 3, NSA_HEADS)
    q_rot = _rope(q, pos, NSA_ROT).reshape(B, T, NSA_KV, NSA_GROUP, NSA_HD)
    q = q.reshape(B, T, NSA_KV, NSA_GROUP, NSA_HD)
    return (q, q_rot, kc, vc, _rope(ks, pos, NSA_ROT), vs, _rope(kw, pos, NSA_ROT), vw,
            gates.astype(x.dtype))


def _nsa_merge(o_cmp, o_slc, o_win, gates, w_o):
    B, T = o_cmp.shape[:2]
    o = (gates[:, :, 0, :, None] * o_cmp.reshape(B, T, NSA_HEADS, NSA_HD)
         + gates[:, :, 1, :, None] * o_slc.reshape(B, T, NSA_HEADS, NSA_HD)
         + gates[:, :, 2, :, None] * o_win.reshape(B, T, NSA_HEADS, NSA_HD))
    return o.reshape(B, T, NSA_Q) @ w_o


def _nsa_prompt(x, w_in, w_o, cmp_k, cmp_v):
    S = x.shape[1]
    pos = jnp.arange(S)
    q, q_rot, kc, vc, ks, vs, kw, vw, gates = _nsa_project(x, pos, w_in)
    k_blk = _cmp_combine(_cmp_chunk_proj(kc, cmp_k[1]), *cmp_k)
    v_blk = _cmp_combine(_cmp_chunk_proj(vc, cmp_v[1]), *cmp_v)
    o_cmp, idx = _nsa_compressed(q, k_blk, v_blk, pos, S // SLC_LEN)
    o_slc = _slc_prompt(q_rot, ks, vs, idx)
    o_win = _band_prompt(q_rot, kw, vw, NSA_WINDOW)
    lw = min(NSA_WINDOW, S)
    return _nsa_merge(o_cmp, o_slc, o_win, gates, w_o), (kc, vc, ks, vs, kw[:, S - lw:], vw[:, S - lw:])


def _nsa_decode(x, layer, pool_kc, pool_vc, pool_ks, pool_vs, buf_k, buf_v, page_table, w_in, w_o, cmp_k, cmp_v):
    B, T, _ = x.shape
    pos = PAST_LEN + jnp.arange(T)
    q, q_rot, kc, vc, ks, vs, kw, vw, gates = _nsa_project(x, pos, w_in)
    pad = (-T) % CMP_STRIDE

    def blocks(pool, new, prm):
        past = pool[layer, page_table].reshape(B, PAST_LEN, NSA_KV, NSA_HD)
        new = jnp.pad(new, ((0, 0), (0, pad), (0, 0), (0, 0)))
        projs = [jnp.concatenate([a, b], 1)
                 for a, b in zip(_cmp_chunk_proj(past, prm[1]), _cmp_chunk_proj(new, prm[1]))]
        return _cmp_combine(projs, *prm)

    n_slc = PAST_LEN // SLC_LEN + (-(-T // SLC_LEN))
    o_cmp, idx = _nsa_compressed(q, blocks(pool_kc, kc, cmp_k), blocks(pool_vc, vc, cmp_v), pos, n_slc)
    o_slc = _slc_decode(q_rot, ks, vs, pool_ks, pool_vs, layer, page_table, idx, pos)
    o_win, wk, wv = _band_decode(q_rot, kw, vw, buf_k, buf_v, NSA_WINDOW)
    return _nsa_merge(o_cmp, o_slc, o_win, gates, w_o), (kc, vc, ks, vs, wk, wv)


def _swa_project(x, pos, w_in, b_in):
    B, T, _ = x.shape
    h = x @ w_in + b_in
    q = _rope(h[..., :SWA_Q].reshape(B, T, SWA_HEADS, SWA_HD), pos, SWA_ROT)
    k = _rope(h[..., SWA_Q:SWA_Q + SWA_KVD].reshape(B, T, SWA_KV, SWA_HD), pos, SWA_ROT)
    v = h[..., SWA_Q + SWA_KVD:].reshape(B, T, SWA_KV, SWA_HD)
    return q.reshape(B, T, SWA_KV, SWA_GROUP, SWA_HD), k, v


def _sink(sinks):
    return sinks.astype(jnp.float32).reshape(SWA_KV, SWA_GROUP, 1, 1)


def _swa_prompt(x, w_in, b_in, sinks, w_o):
    B, S, _ = x.shape
    q, k, v = _swa_project(x, jnp.arange(S), w_in, b_in)
    o = _band_prompt(q, k, v, SWA_WINDOW, _sink(sinks))
    lw = min(SWA_WINDOW, S)
    return o.reshape(B, S, SWA_Q) @ w_o, (k[:, S - lw:], v[:, S - lw:])


def _swa_decode(x, buf_k, buf_v, w_in, b_in, sinks, w_o):
    B, T, _ = x.shape
    q, k, v = _swa_project(x, PAST_LEN + jnp.arange(T), w_in, b_in)
    o, nk, nv = _band_decode(q, k, v, buf_k, buf_v, SWA_WINDOW, _sink(sinks))
    return o.reshape(B, T, SWA_Q) @ w_o, (nk, nv)


def _ffn(h_ext, conv_w, conv_b, w_down):
    T = h_ext.shape[1] - (CONV_W - 1)
    c = conv_b
    for j in range(CONV_W):
        c = c + h_ext[:, j:j + T] * conv_w[j]
    return (c[..., :D_FF] * jax.nn.gelu(c[..., D_FF:])) @ w_down


def _ffn_prompt(x, w_up, conv_w, conv_b, w_down):
    h = x @ w_up
    h_ext = jnp.pad(h, ((0, 0), (CONV_W - 1, 0), (0, 0)))
    return _ffn(h_ext, conv_w, conv_b, w_down), h_ext[:, -(CONV_W - 1):]


def _ffn_decode(x, buf, w_up, conv_w, conv_b, w_down):
    h_ext = jnp.concatenate([buf, x @ w_up], 1)
    return _ffn(h_ext, conv_w, conv_b, w_down), h_ext[:, -(CONV_W - 1):]


def setup_inputs(seed: int = 0) -> dict:
    key = jax.random.key(seed)
    ks = iter(jax.random.split(key, 48))

    def nrm(shape, scale=1.0):
        return jax.random.normal(next(ks), shape, jnp.float32) * scale

    n_pages = PAST_LEN // PAGE_SIZE
    n_pool = (5 * DEC_BATCH * n_pages) // 4
    nsa_wbuf = min(NSA_WINDOW, PAST_LEN)
    swa_wbuf = min(SWA_WINDOW, PAST_LEN)
    pool = (N_NSA, n_pool, PAGE_SIZE, NSA_KV, NSA_HD)
    return {
        'x_prompt': nrm((BATCH, SEQ, D_MODEL)),
        'x_sample': nrm((DEC_BATCH, DEC_SEQ, D_MODEL)),
        'cache_nsa_cmp_k': nrm(pool),
        'cache_nsa_cmp_v': nrm(pool),
        'cache_nsa_slc_k': nrm(pool),
        'cache_nsa_slc_v': nrm(pool),
        'state_nsa_win_k': nrm((N_NSA, DEC_BATCH, nsa_wbuf, NSA_KV, NSA_HD)),
        'state_nsa_win_v': nrm((N_NSA, DEC_BATCH, nsa_wbuf, NSA_KV, NSA_HD)),
        'state_swa_k': nrm((N_SWA, DEC_BATCH, swa_wbuf, SWA_KV, SWA_HD)),
        'state_swa_v': nrm((N_SWA, DEC_BATCH, swa_wbuf, SWA_KV, SWA_HD)),
        'state_conv': nrm((DEPTH, DEC_BATCH, CONV_W - 1, 2 * D_FF)),
        'page_table': jax.random.permutation(next(ks), n_pool)[:DEC_BATCH * n_pages]
                      .reshape(DEC_BATCH, n_pages).astype(jnp.int32),
        'nsa_w_in': nrm((N_NSA, D_MODEL, NSA_IN), D_MODEL ** -0.5),
        'nsa_w_o': nrm((N_NSA, NSA_Q, D_MODEL), BETA * NSA_Q ** -0.5),
        'nsa_cmp_pe_k': nrm((N_NSA, CMP_LEN, NSA_HD), 0.1),
        'nsa_cmp_w1_k': nrm((N_NSA, CMP_LEN, NSA_HD, CMP_HID), (CMP_LEN * NSA_HD) ** -0.5),
        'nsa_cmp_b1_k': nrm((N_NSA, CMP_HID), 0.01),
        'nsa_cmp_w2_k': nrm((N_NSA, CMP_HID, NSA_HD), CMP_HID ** -0.5),
        'nsa_cmp_pe_v': nrm((N_NSA, CMP_LEN, NSA_HD), 0.1),
        'nsa_cmp_w1_v': nrm((N_NSA, CMP_LEN, NSA_HD, CMP_HID), (CMP_LEN * NSA_HD) ** -0.5),
        'nsa_cmp_b1_v': nrm((N_NSA, CMP_HID), 0.01),
        'nsa_cmp_w2_v': nrm((N_NSA, CMP_HID, NSA_HD), CMP_HID ** -0.5),
        'swa_w_in': nrm((N_SWA, D_MODEL, SWA_IN), D_MODEL ** -0.5),
        'swa_b_in': nrm((N_SWA, SWA_IN), 0.01),
        'swa_sinks': nrm((N_SWA, SWA_HEADS), 1.0),
        'swa_w_o': nrm((N_SWA, SWA_Q, D_MODEL), BETA * SWA_Q ** -0.5),
        'ln1_g': 1.0 + nrm((DEPTH, D_MODEL), 0.02),
        'ln1_b': nrm((DEPTH, D_MODEL), 0.02),
        'ln2_g': 1.0 + nrm((DEPTH, D_MODEL), 0.02),
        'ln2_b': nrm((DEPTH, D_MODEL), 0.02),
        'ffn_w_up': nrm((DEPTH, D_MODEL, 2 * D_FF), D_MODEL ** -0.5),
        'ffn_conv_w': nrm((DEPTH, CONV_W, 2 * D_FF), CONV_W ** -0.5),
        'ffn_conv_b': nrm((DEPTH, 2 * D_FF), 0.01),
        'ffn_w_down': nrm((DEPTH, D_FF, D_MODEL), BETA * D_FF ** -0.5),
    }


def reference(x_prompt, x_sample, cache_nsa_cmp_k, cache_nsa_cmp_v, cache_nsa_slc_k, cache_nsa_slc_v,
              state_nsa_win_k, state_nsa_win_v, state_swa_k, state_swa_v, state_conv, page_table,
              nsa_w_in, nsa_w_o, nsa_cmp_pe_k, nsa_cmp_w1_k, nsa_cmp_b1_k, nsa_cmp_w2_k,
              nsa_cmp_pe_v, nsa_cmp_w1_v, nsa_cmp_b1_v, nsa_cmp_w2_v,
              swa_w_in, swa_b_in, swa_sinks, swa_w_o, ln1_g, ln1_b, ln2_g, ln2_b,
              ffn_w_up, ffn_conv_w, ffn_conv_b, ffn_w_down):
    nsa_p, nsa_s, swa_p, swa_s, conv_p, conv_s = [], [], [], [], [], []
    yp, ys = x_prompt, x_sample
    for i in range(DEPTH):
        a = i // 2
        if i % 2 == 0:
            cmp_k = (nsa_cmp_pe_k[a], nsa_cmp_w1_k[a], nsa_cmp_b1_k[a], nsa_cmp_w2_k[a])
            cmp_v = (nsa_cmp_pe_v[a], nsa_cmp_w1_v[a], nsa_cmp_b1_v[a], nsa_cmp_w2_v[a])
            mp, st = _nsa_prompt(yp, nsa_w_in[a], nsa_w_o[a], cmp_k, cmp_v)
            nsa_p.append(st)
            ms, st = _nsa_decode(ys, a, cache_nsa_cmp_k, cache_nsa_cmp_v, cache_nsa_slc_k, cache_nsa_slc_v,
                                 state_nsa_win_k[a], state_nsa_win_v[a], page_table,
                                 nsa_w_in[a], nsa_w_o[a], cmp_k, cmp_v)
            nsa_s.append(st)
        else:
            mp, st = _swa_prompt(yp, swa_w_in[a], swa_b_in[a], swa_sinks[a], swa_w_o[a])
            swa_p.append(st)
            ms, st = _swa_decode(ys, state_swa_k[a], state_swa_v[a], swa_w_in[a], swa_b_in[a], swa_sinks[a], swa_w_o[a])
            swa_s.append(st)
        yp = _layernorm(ALPHA * yp + mp, ln1_g[i], ln1_b[i])
        ys = _layernorm(ALPHA * ys + ms, ln1_g[i], ln1_b[i])
        fp, st = _ffn_prompt(yp, ffn_w_up[i], ffn_conv_w[i], ffn_conv_b[i], ffn_w_down[i])
        conv_p.append(st)
        fs, st = _ffn_decode(ys, state_conv[i], ffn_w_up[i], ffn_conv_w[i], ffn_conv_b[i], ffn_w_down[i])
        conv_s.append(st)
        yp = _layernorm(ALPHA * yp + fp, ln2_g[i], ln2_b[i])
        ys = _layernorm(ALPHA * ys + fs, ln2_g[i], ln2_b[i])
    p_cmp_k, p_cmp_v, p_slc_k, p_slc_v, p_nwin_k, p_nwin_v = [jnp.stack(t) for t in zip(*nsa_p)]
    s_cmp_k, s_cmp_v, s_slc_k, s_slc_v, s_nwin_k, s_nwin_v = [jnp.stack(t) for t in zip(*nsa_s)]
    p_swa_k, p_swa_v = [jnp.stack(t) for t in zip(*swa_p)]
    s_swa_k, s_swa_v = [jnp.stack(t) for t in zip(*swa_s)]
    p_conv = jnp.stack(conv_p)
    s_conv = jnp.stack(conv_s)
    return (yp, ys, p_cmp_k, p_cmp_v, p_slc_k, p_slc_v, p_nwin_k, p_nwin_v, p_swa_k, p_swa_v, p_conv,
            s_cmp_k, s_cmp_v, s_slc_k, s_slc_v, s_nwin_k, s_nwin_v, s_swa_k, s_swa_v, s_conv)
```

```python
import functools
import math

import jax
import jax.numpy as jnp
from jax import lax
from jax.experimental import pallas as pl
from jax.experimental.pallas import tpu as pltpu

F32 = jnp.float32
BF16 = jnp.bfloat16

D_MODEL = 2048
BATCH = 4
SEQ = 2048
DEPTH = 4
DEC_BATCH = 32
PAST_LEN = 16384
PAGE_SIZE = 128

NSA_HEADS = 16
NSA_KV = 4
NSA_HD = 128
NSA_GROUP = 4
NSA_ROT = 32
NSA_Q = 2048
NSA_KVD = 512
NSA_MAIN = NSA_Q + 6 * NSA_KVD
CMP_LEN = 32
CMP_STRIDE = 16
CMP_HID = 256
SLC_LEN = 64
SLC_TOPK = 16
NSA_WINDOW = 512
FORCE_SCORE = 1e4
SWA_HEADS = 32
SWA_KV = 4
SWA_HD = 64
SWA_GROUP = 8
SWA_ROT = 16
SWA_Q = 2048
SWA_KVD = 256
SWA_WINDOW = 128
ROPE_THETA = 500000.0
D_FF = 5632
ALPHA = (2 * DEPTH) ** 0.25
LN_EPS = 1e-5
NEG = -1e30

LANES = 128
PAGES_PER_STEP = 32
CHUNK_PITCH = 72
N_PAGES = PAST_LEN // PAGE_SIZE
N_PAST_CHUNK = PAST_LEN // CMP_STRIDE
N_PAST_BLK = PAST_LEN // SLC_LEN
N_SLC_DEC = N_PAST_BLK + 1
SLC_LANES = 384
VMEM_LIMIT = 56 * 1024 * 1024


def _cparams(sem):
    return pltpu.CompilerParams(dimension_semantics=sem, vmem_limit_bytes=VMEM_LIMIT)


def _ln(z, g, b):
    mu = jnp.mean(z, axis=-1, keepdims=True)
    d = z - mu
    var = jnp.mean(d * d, axis=-1, keepdims=True)
    return d * lax.rsqrt(var + LN_EPS) * g + b


def _mm_body(*refs, has_bias):
    if has_bias:
        x_ref, w_ref, b_ref, o_ref = refs
    else:
        x_ref, w_ref, o_ref = refs
    acc = jnp.dot(x_ref[...], w_ref[...], preferred_element_type=F32)
    if has_bias:
        acc = acc + b_ref[...]
    o_ref[...] = acc


def _mm(x, w, bias=None, *, tm, tn):
    M, K = x.shape
    N = w.shape[1]
    in_specs = [pl.BlockSpec((tm, K), lambda i, j: (i, 0)),
                pl.BlockSpec((K, tn), lambda i, j: (0, j))]
    args = [x, w]
    if bias is not None:
        in_specs.append(pl.BlockSpec((1, tn), lambda i, j: (0, j)))
        args.append(bias)
    return pl.pallas_call(
        functools.partial(_mm_body, has_bias=bias is not None),
        out_shape=jax.ShapeDtypeStruct((M, N), F32),
        grid=(M // tm, N // tn),
        in_specs=in_specs,
        out_specs=pl.BlockSpec((tm, tn), lambda i, j: (i, j)),
        compiler_params=_cparams(("parallel", "arbitrary")),
        name="mm",
    )(*args)


def _out_ln_body(*refs, n_lhs, gated, nk):
    lhs_refs = refs[:n_lhs]
    pos = n_lhs
    gate_ref = None
    if gated:
        gate_ref = refs[pos]
        pos += 1
    w_ref, res_ref, g_ref, b_ref, y_ref, acc_ref = refs[pos:pos + 6]
    k = pl.program_id(1)
    if gated:
        sig = jax.nn.sigmoid(gate_ref[...])
        vals = [r[...] for r in lhs_refs]
        cols = []
        for h in range(NSA_HEADS):
            sl = slice(h * NSA_HD, (h + 1) * NSA_HD)
            m = None
            for br in range(3):
                c = br * NSA_HEADS + h
                t = sig[:, c:c + 1] * vals[br][:, sl]
                m = t if m is None else m + t
            cols.append(m)
        lhs = jnp.concatenate(cols, axis=1).astype(BF16)
    else:
        lhs = lhs_refs[0][...].astype(BF16)
    contrib = jnp.dot(lhs, w_ref[...], preferred_element_type=F32)

    @pl.when(k == 0)
    def _():
        acc_ref[...] = contrib

    @pl.when(k > 0)
    def _():
        acc_ref[...] += contrib

    @pl.when(k == nk - 1)
    def _():
        y_ref[...] = _ln(ALPHA * res_ref[...] + acc_ref[...], g_ref[...], b_ref[...])


def _out_ln(lhs_list, gate, w, res, g, b, *, tm, tk):
    M, K = lhs_list[0].shape
    N = w.shape[1]
    nk = K // tk
    gated = gate is not None
    assert not gated or nk == 1
    in_specs = [pl.BlockSpec((tm, tk), lambda i, k: (i, k)) for _ in lhs_list]
    args = list(lhs_list)
    if gated:
        in_specs.append(pl.BlockSpec((tm, LANES), lambda i, k: (i, 0)))
        args.append(gate)
    in_specs += [pl.BlockSpec((tk, N), lambda i, k: (k, 0)),
                 pl.BlockSpec((tm, N), lambda i, k: (i, 0)),
                 pl.BlockSpec((1, N), lambda i, k: (0, 0)),
                 pl.BlockSpec((1, N), lambda i, k: (0, 0))]
    args += [w, res, g, b]
    return pl.pallas_call(
        functools.partial(_out_ln_body, n_lhs=len(lhs_list), gated=gated, nk=nk),
        out_shape=jax.ShapeDtypeStruct((M, N), F32),
        grid=(M // tm, nk),
        in_specs=in_specs,
        out_specs=pl.BlockSpec((tm, N), lambda i, k: (i, 0)),
        scratch_shapes=[pltpu.VMEM((tm, N), F32)],
        compiler_params=_cparams(("parallel", "arbitrary")),
        name="out_ln",
    )(*args)


def _ffn_body(xb_ref, xr_ref, wa_ref, wg_ref, cwa_ref, cwg_ref, cba_ref, cbg_ref, wd_ref, g_ref, b_ref,
              y_ref, sa_ref, sg_ref, acc_ref, carry_a, carry_g, *, tiles_per_seq, nj, tm):
    i = pl.program_id(0)
    j = pl.program_id(1)
    first = (i % tiles_per_seq) == 0
    x = xb_ref[...]

    @pl.when(first)
    def _():
        carry_a[j] = jnp.zeros(carry_a.shape[1:], F32)
        carry_g[j] = jnp.zeros(carry_g.shape[1:], F32)

    def branch(w_ref, cw_ref, cb_ref, carry, st_ref):
        h = jnp.dot(x, w_ref[...], preferred_element_type=F32)
        prev = carry[j]
        row = lax.broadcasted_iota(jnp.int32, h.shape, 0)
        h1 = jnp.where(row == 0, prev[7:8], pltpu.roll(h, 1, 0))
        h2 = jnp.where(row == 0, prev[6:7], jnp.where(row == 1, prev[7:8], pltpu.roll(h, 2, 0)))
        cw = cw_ref[...]
        c = cb_ref[...] + h2 * cw[0:1] + h1 * cw[1:2] + h * cw[2:3]
        last = h[tm - 8:tm]
        carry[j] = last
        st_ref[0] = last
        return c

    ca = branch(wa_ref, cwa_ref, cba_ref, carry_a, sa_ref)
    cg = branch(wg_ref, cwg_ref, cbg_ref, carry_g, sg_ref)
    gated = (ca * jax.nn.gelu(cg)).astype(BF16)
    contrib = jnp.dot(gated, wd_ref[...], preferred_element_type=F32)

    @pl.when(j == 0)
    def _():
        acc_ref[...] = contrib

    @pl.when(j > 0)
    def _():
        acc_ref[...] += contrib

    @pl.when(j == nj - 1)
    def _():
        y_ref[...] = _ln(ALPHA * xr_ref[...] + acc_ref[...], g_ref[...], b_ref[...])


def _ffn_prompt(xb, xr, w_up, conv_w, conv_b, w_down, g, b, *, tm=512, tc=512):
    M = xb.shape[0]
    nj = D_FF // tc
    nb = M // tm
    tiles_per_seq = SEQ // tm
    y, sa, sg = pl.pallas_call(
        functools.partial(_ffn_body, tiles_per_seq=tiles_per_seq, nj=nj, tm=tm),
        out_shape=(jax.ShapeDtypeStruct((M, D_MODEL), F32),
                   jax.ShapeDtypeStruct((nb, 8, D_FF), F32),
                   jax.ShapeDtypeStruct((nb, 8, D_FF), F32)),
        grid=(M // tm, nj),
        in_specs=[pl.BlockSpec((tm, D_MODEL), lambda i, j: (i, 0)),
                  pl.BlockSpec((tm, D_MODEL), lambda i, j: (i, 0)),
                  pl.BlockSpec((D_MODEL, tc), lambda i, j: (0, j)),
                  pl.BlockSpec((D_MODEL, tc), lambda i, j: (0, j + nj)),
                  pl.BlockSpec((3, tc), lambda i, j: (0, j)),
                  pl.BlockSpec((3, tc), lambda i, j: (0, j + nj)),
                  pl.BlockSpec((1, tc), lambda i, j: (0, j)),
                  pl.BlockSpec((1, tc), lambda i, j: (0, j + nj)),
                  pl.BlockSpec((tc, D_MODEL), lambda i, j: (j, 0)),
                  pl.BlockSpec((1, D_MODEL), lambda i, j: (0, 0)),
                  pl.BlockSpec((1, D_MODEL), lambda i, j: (0, 0))],
        out_specs=(pl.BlockSpec((tm, D_MODEL), lambda i, j: (i, 0)),
                   pl.BlockSpec((1, 8, tc), lambda i, j: (i, 0, j)),
                   pl.BlockSpec((1, 8, tc), lambda i, j: (i, 0, j))),
        scratch_shapes=[pltpu.VMEM((tm, D_MODEL), F32),
                        pltpu.VMEM((nj, 8, tc), F32),
                        pltpu.VMEM((nj, 8, tc), F32)],
        compiler_params=_cparams(("arbitrary", "arbitrary")),
        name="ffn",
    )(xb, xr, w_up, w_up, conv_w, conv_w, conv_b, conv_b, w_down, g, b)
    return y, sa, sg


def _dec_gate_body(ha_ref, hg_ref, s0a_ref, s0g_ref, s1a_ref, s1g_ref, cwa_ref, cwg_ref, cba_ref, cbg_ref, o_ref):
    def conv(h_ref, s0_ref, s1_ref, cw_ref, cb_ref):
        cw = cw_ref[...]
        return cb_ref[...] + s0_ref[...] * cw[0:1] + s1_ref[...] * cw[1:2] + h_ref[...] * cw[2:3]

    ca = conv(ha_ref, s0a_ref, s1a_ref, cwa_ref, cba_ref)
    cg = conv(hg_ref, s0g_ref, s1g_ref, cwg_ref, cbg_ref)
    o_ref[...] = (ca * jax.nn.gelu(cg)).astype(BF16)


def _dec_gate(h, s0, s1, conv_w, conv_b, *, tc=512):
    nb = h.shape[0]
    nj = D_FF // tc
    a_spec = pl.BlockSpec((nb, tc), lambda j: (0, j))
    g_spec = pl.BlockSpec((nb, tc), lambda j: (0, j + nj))
    return pl.pallas_call(
        _dec_gate_body,
        out_shape=jax.ShapeDtypeStruct((nb, D_FF), BF16),
        grid=(nj,),
        in_specs=[a_spec, g_spec, a_spec, g_spec, a_spec, g_spec,
                  pl.BlockSpec((3, tc), lambda j: (0, j)), pl.BlockSpec((3, tc), lambda j: (0, j + nj)),
                  pl.BlockSpec((1, tc), lambda j: (0, j)), pl.BlockSpec((1, tc), lambda j: (0, j + nj))],
        out_specs=pl.BlockSpec((nb, tc), lambda j: (0, j)),
        compiler_params=_cparams(("arbitrary",)),
        name="dec_gate",
    )(h, h, s0, s0, s1, s1, conv_w, conv_w, conv_b, conv_b)


def _rope_tables(pos, rot, hd):
    half = rot // 2
    inv = ROPE_THETA ** (-jnp.arange(half, dtype=F32) / half)
    ang = pos.astype(F32)[:, None] * inv[None, :]
    cos, sin = jnp.cos(ang), jnp.sin(ang)
    n = pos.shape[0]
    ones = jnp.ones((n, hd - rot), F32)
    zeros_r = jnp.zeros((n, hd - rot), F32)
    zeros_h = jnp.zeros((n, half), F32)
    c = jnp.concatenate([cos, cos, ones], axis=1)
    s1 = jnp.concatenate([zeros_h, sin, zeros_r], axis=1)
    s2 = jnp.concatenate([-sin, zeros_h, zeros_r], axis=1)
    reps = LANES // hd
    return tuple(jnp.tile(t, (1, reps)) for t in (c, s1, s2))


def _rope_body(x_ref, c_ref, s1_ref, s2_ref, o_ref, *, half):
    x = x_ref[...]
    w = x.shape[1]
    reps = w // LANES
    c = jnp.concatenate([c_ref[...]] * reps, axis=1)
    s1 = jnp.concatenate([s1_ref[...]] * reps, axis=1)
    s2 = jnp.concatenate([s2_ref[...]] * reps, axis=1)
    o_ref[...] = x * c + pltpu.roll(x, half, 1) * s1 + pltpu.roll(x, w - half, 1) * s2


def _rope(h, col0, width, tabs, *, half, tr, rows_per_seq):
    M = h.shape[0]
    cb = col0 // width
    nt = rows_per_seq // tr
    tab_spec = pl.BlockSpec((tr, LANES), lambda i: (i % nt, 0))
    return pl.pallas_call(
        functools.partial(_rope_body, half=half),
        out_shape=jax.ShapeDtypeStruct((M, width), F32),
        grid=(M // tr,),
        in_specs=[pl.BlockSpec((tr, width), lambda i: (i, cb)), tab_spec, tab_spec, tab_spec],
        out_specs=pl.BlockSpec((tr, width), lambda i: (i, 0)),
        compiler_params=_cparams(("parallel",)),
        name="rope",
    )(h, *tabs)


def _attn_body(*refs, R, hd, tq, tk, window, use_sel, use_sink, tok_layout):
    it = iter(refs)
    q_ref, k_ref, v_ref = next(it), next(it), next(it)
    sel_ref = next(it) if use_sel else None
    e_ref = next(it) if use_sel else None
    sink_ref = next(it) if use_sink else None
    o_ref = next(it)
    m_sc, l_sc, acc_sc = next(it), next(it), next(it)

    i = pl.program_id(2)
    t0 = i * tq
    if tok_layout:
        qv = q_ref[...]
        q = jnp.concatenate([qv[:, r * hd:(r + 1) * hd] for r in range(R)], axis=0).astype(BF16)
    else:
        q = q_ref[0, 0].reshape(R * tq, hd)
    scale = hd ** -0.5
    qpos = t0 + lax.broadcasted_iota(jnp.int32, (tq, tk), 0)
    lane = lax.broadcasted_iota(jnp.int32, (tq, tk), 1)
    if use_sel:
        selv = sel_ref[0, 0].astype(BF16)

    if use_sink:
        sk = sink_ref[0][:, 0:1, 0:1]
        m_sc[...] = jnp.broadcast_to(sk, (R, tq, 1))
        l_sc[...] = jnp.ones((R, tq, 1), F32)
    else:
        m_sc[...] = jnp.full((R, tq, 1), NEG, F32)
        l_sc[...] = jnp.zeros((R, tq, 1), F32)
    acc_sc[...] = jnp.zeros((R * tq, hd), F32)

    lo = jnp.maximum(t0 - (window - 1), 0) // tk if window is not None else 0
    hi = (t0 + tq - 1) // tk

    def step(kt, carry):
        k0 = pl.multiple_of(kt * tk, tk)
        if tok_layout:
            kb = k_ref[pl.ds(k0, tk), :].astype(BF16)
            vb = v_ref[pl.ds(k0, tk), :].astype(BF16)
        else:
            kb = k_ref[0, 0, pl.ds(k0, tk), :]
            vb = v_ref[0, 0, pl.ds(k0, tk), :]
        s = lax.dot_general(q, kb, (((1,), (1,)), ((), ())), preferred_element_type=F32) * scale
        kpos = k0 + lane
        mask = kpos <= qpos
        if window is not None:
            mask = mask & ((qpos - kpos) < window)
        if use_sel:
            selx = jnp.dot(selv, e_ref[:, pl.ds(k0, tk)], preferred_element_type=F32)
            mask = mask & (selx > 0.5)
        mask3 = mask[None]
        s3 = jnp.where(mask3, s.reshape(R, tq, tk), NEG)
        m_old = m_sc[...]
        m_new = jnp.maximum(m_old, jnp.max(s3, axis=-1, keepdims=True))
        alpha = jnp.exp(m_old - m_new)
        p = jnp.where(mask3, jnp.exp(s3 - m_new), 0.0)
        l_sc[...] = alpha * l_sc[...] + jnp.sum(p, axis=-1, keepdims=True)
        m_sc[...] = m_new
        pv = jnp.dot(p.reshape(R * tq, tk).astype(BF16), vb, preferred_element_type=F32)
        acc_sc[...] = alpha.reshape(R * tq, 1) * acc_sc[...] + pv
        return carry

    lax.fori_loop(lo, hi + 1, step, 0)
    o = acc_sc[...] / l_sc[...].reshape(R * tq, 1)
    if tok_layout:
        o_ref[...] = jnp.concatenate([o[r * tq:(r + 1) * tq] for r in range(R)], axis=1)
    else:
        o_ref[0, 0] = o.reshape(R, tq, hd)


def _sel_expand_matrix(seq):
    j = jnp.arange(LANES)[:, None]
    k = jnp.arange(seq)[None, :]
    return (k // SLC_LEN == j).astype(BF16)


def _attn_tok(q, q_cb, k, k_cb, v, v_cb, sel, *, window, tq, tk):
    R, hd, G = NSA_GROUP, NSA_HD, NSA_KV
    M = q.shape[0]
    nb = M // SEQ
    nq = SEQ // tq
    use_sel = sel is not None
    in_specs = [pl.BlockSpec((tq, R * hd), lambda b, g, i: (b * nq + i, q_cb + g)),
                pl.BlockSpec((SEQ, hd), lambda b, g, i: (b, k_cb + g)),
                pl.BlockSpec((SEQ, hd), lambda b, g, i: (b, v_cb + g))]
    args = [q, k, v]
    if use_sel:
        in_specs += [pl.BlockSpec((1, 1, tq, LANES), lambda b, g, i: (b, g, i, 0)),
                     pl.BlockSpec((LANES, SEQ), lambda b, g, i: (0, 0))]
        args += [sel, _sel_expand_matrix(SEQ)]
    return pl.pallas_call(
        functools.partial(_attn_body, R=R, hd=hd, tq=tq, tk=tk, window=window, use_sel=use_sel,
                          use_sink=False, tok_layout=True),
        out_shape=jax.ShapeDtypeStruct((M, NSA_Q), F32),
        grid=(nb, G, nq),
        in_specs=in_specs,
        out_specs=pl.BlockSpec((tq, R * hd), lambda b, g, i: (b * nq + i, g)),
        scratch_shapes=[pltpu.VMEM((R, tq, 1), F32), pltpu.VMEM((R, tq, 1), F32), pltpu.VMEM((R * tq, hd), F32)],
        compiler_params=_cparams(("parallel", "parallel", "arbitrary")),
        name="attn_nsa",
    )(*args)


def _attn_head(q, k, v, sinks, *, window, tq, tk):
    nb, G, R, S, hd = q.shape
    nq = S // tq
    return pl.pallas_call(
        functools.partial(_attn_body, R=R, hd=hd, tq=tq, tk=tk, window=window, use_sel=False,
                          use_sink=True, tok_layout=False),
        out_shape=jax.ShapeDtypeStruct((nb, G, R, S, hd), F32),
        grid=(nb, G, nq),
        in_specs=[pl.BlockSpec((1, 1, R, tq, hd), lambda b, g, i: (b, g, 0, i, 0)),
                  pl.BlockSpec((1, 1, S, hd), lambda b, g, i: (b, g, 0, 0)),
                  pl.BlockSpec((1, 1, S, hd), lambda b, g, i: (b, g, 0, 0)),
                  pl.BlockSpec((1, R, 8, LANES), lambda b, g, i: (g, 0, 0, 0))],
        out_specs=pl.BlockSpec((1, 1, R, tq, hd), lambda b, g, i: (b, g, 0, i, 0)),
        scratch_shapes=[pltpu.VMEM((R, tq, 1), F32), pltpu.VMEM((R, tq, 1), F32), pltpu.VMEM((R * tq, hd), F32)],
        compiler_params=_cparams(("parallel", "parallel", "arbitrary")),
        name="attn_swa",
    )(q, k, v, sinks)


def _cmp_base(pe_ref, wc, b1_ref):
    base = (jnp.dot(pe_ref[0], wc[:, :CMP_HID], preferred_element_type=F32)
            + jnp.dot(pe_ref[1], wc[:, CMP_HID:], preferred_element_type=F32))
    return base[0:1] + b1_ref[...]


def _cmp_mlp_body(x_ref, wc_ref, pe_ref, b1_ref, w2_ref, o_ref, *, n_chunk):
    pieces = [x_ref[pl.ds(s, n_chunk, stride=CMP_STRIDE), :].astype(BF16) for s in range(CMP_STRIDE)]
    x = jnp.concatenate(pieces, axis=1)
    wc = wc_ref[0]
    p = jnp.dot(x, wc, preferred_element_type=F32)
    base = _cmp_base(pe_ref.at[0], wc, b1_ref.at[0])
    p1s = pltpu.roll(p[:, CMP_HID:], n_chunk - 1, 0)
    h = base + p[:, :CMP_HID] + p1s
    kb = jnp.dot(jax.nn.gelu(h).astype(BF16), w2_ref[0], preferred_element_type=F32)
    o_ref[0, 0, 0] = kb.astype(BF16)


def _cmp_mlp_prompt(h, wcat, pe2, b1, w2):
    M = h.shape[0]
    nb = M // SEQ
    n_chunk = SEQ // CMP_STRIDE
    cb0 = NSA_Q // NSA_HD
    return pl.pallas_call(
        functools.partial(_cmp_mlp_body, n_chunk=n_chunk),
        out_shape=jax.ShapeDtypeStruct((2, nb, NSA_KV, n_chunk, NSA_HD), BF16),
        grid=(2, nb, NSA_KV),
        in_specs=[pl.BlockSpec((SEQ, NSA_HD), lambda c, b, g: (b, cb0 + 4 * c + g)),
                  pl.BlockSpec((1, CMP_STRIDE * NSA_HD, 2 * CMP_HID), lambda c, b, g: (c, 0, 0)),
                  pl.BlockSpec((1, 2, 8, CMP_STRIDE * NSA_HD), lambda c, b, g: (c, 0, 0, 0)),
                  pl.BlockSpec((1, 1, CMP_HID), lambda c, b, g: (c, 0, 0)),
                  pl.BlockSpec((1, CMP_HID, NSA_HD), lambda c, b, g: (c, 0, 0))],
        out_specs=pl.BlockSpec((1, 1, 1, n_chunk, NSA_HD), lambda c, b, g: (c, b, g, 0, 0)),
        compiler_params=_cparams(("parallel", "parallel", "parallel")),
        name="cmp_mlp",
    )(h, wcat, pe2, b1, w2)


def _cmp_attn_body(q_ref, kb_ref, vb_ref, ov_ref, o_ref, sel_ref, *, tq, n_cmp, n_slc):
    R, hd = NSA_GROUP, NSA_HD
    i = pl.program_id(2)
    t0 = i * tq
    qv = q_ref[...]
    q = jnp.concatenate([qv[:, r * hd:(r + 1) * hd] for r in range(R)], axis=0).astype(BF16)
    s = lax.dot_general(q, kb_ref[0, 0, 0], (((1,), (1,)), ((), ())), preferred_element_type=F32) * (hd ** -0.5)
    n = lax.broadcasted_iota(jnp.int32, (tq, LANES), 1)
    qpos = t0 + lax.broadcasted_iota(jnp.int32, (tq, LANES), 0)
    mask = ((n * CMP_STRIDE + CMP_LEN - 1) <= qpos) & (n < n_cmp)
    mask3 = mask[None]
    s3 = jnp.where(mask3, s.reshape(R, tq, LANES), NEG)
    m = jnp.max(s3, axis=-1, keepdims=True)
    e = jnp.where(mask3, jnp.exp(s3 - m), 0.0)
    l = jnp.sum(e, axis=-1, keepdims=True)
    p = e / jnp.where(l > 0.0, l, 1.0)
    o = jnp.dot(p.reshape(R * tq, LANES).astype(BF16), vb_ref[0, 0, 0], preferred_element_type=F32)
    o_ref[...] = jnp.concatenate([o[r * tq:(r + 1) * tq] for r in range(R)], axis=1)

    psum = jnp.sum(p, axis=0)
    imp = jnp.dot(psum.astype(BF16), ov_ref[...], preferred_element_type=F32)
    j = n
    cur = qpos // SLC_LEN
    forced = (j == 0) | (j == cur) | (j == cur - 1)
    imp = jnp.where(forced, FORCE_SCORE, jnp.where(j <= cur, imp, -1.0))
    imp = jnp.where(j < n_slc, imp, -jnp.inf)
    cnt = jnp.zeros((tq, LANES), F32)
    for c in range(n_slc):
        ci = imp[:, c:c + 1]
        before = (ci > imp) | ((ci == imp) & (j > c))
        cnt = cnt + before.astype(F32)
    sel_ref[0, 0] = ((cnt < float(SLC_TOPK)) & (j < n_slc)).astype(F32)


def _overlap_matrix(n_rows, n_cols, n_cmp, n_slc):
    i = jnp.arange(n_rows)[:, None] * CMP_STRIDE
    j = jnp.arange(n_cols)[None, :] * SLC_LEN
    ok = (i < j + SLC_LEN) & (i + CMP_LEN > j) & (jnp.arange(n_rows)[:, None] < n_cmp) & (jnp.arange(n_cols)[None, :] < n_slc)
    return ok.astype(BF16)


def _cmp_attn_prompt(h, kvb, *, tq):
    M = h.shape[0]
    nb = M // SEQ
    nq = SEQ // tq
    n_cmp = SEQ // CMP_STRIDE - 1
    n_slc = SEQ // SLC_LEN
    ov = _overlap_matrix(LANES, LANES, n_cmp, n_slc)
    return pl.pallas_call(
        functools.partial(_cmp_attn_body, tq=tq, n_cmp=n_cmp, n_slc=n_slc),
        out_shape=(jax.ShapeDtypeStruct((M, NSA_Q), F32),
                   jax.ShapeDtypeStruct((nb, NSA_KV, SEQ, LANES), F32)),
        grid=(nb, NSA_KV, nq),
        in_specs=[pl.BlockSpec((tq, NSA_GROUP * NSA_HD), lambda b, g, i: (b * nq + i, g)),
                  pl.BlockSpec((1, 1, 1, LANES, NSA_HD), lambda b, g, i: (0, b, g, 0, 0)),
                  pl.BlockSpec((1, 1, 1, LANES, NSA_HD), lambda b, g, i: (1, b, g, 0, 0)),
                  pl.BlockSpec((LANES, LANES), lambda b, g, i: (0, 0))],
        out_specs=(pl.BlockSpec((tq, NSA_GROUP * NSA_HD), lambda b, g, i: (b * nq + i, g)),
                   pl.BlockSpec((1, 1, tq, LANES), lambda b, g, i: (b, g, i, 0))),
        compiler_params=_cparams(("parallel", "parallel", "arbitrary")),
        name="cmp_attn",
    )(h, kvb, kvb, ov)


def _dec_blocks_body(pt_ref, pool_ref, wc_ref, pe_ref, b1_ref, w2_ref, new_ref, o_ref, buf, sem, carry, *, layer):
    G = NSA_KV
    P = PAGES_PER_STEP
    mrows = 8 * P
    slab = mrows * CHUNK_PITCH
    b = pl.program_id(0)
    j = pl.program_id(1)
    nj = pl.num_programs(1)
    slot = j % 2

    def page_copy(jj, sl, p):
        phys = pt_ref[b, jj * P + p]
        return pltpu.make_async_copy(pool_ref.at[layer, phys],
                                     buf.at[sl, pl.ds(8 * p, 8), pl.ds(0, 64), :], sem.at[sl])

    @pl.when(j == 0)
    def _():
        for p in range(P):
            page_copy(0, 0, p).start()

    @pl.when(j + 1 < nj)
    def _():
        for p in range(P):
            page_copy(j + 1, 1 - slot, p).start()

    for p in range(P):
        page_copy(j, slot, p).wait()

    flat = buf.reshape(2 * slab, LANES)
    base_row = slot * slab
    xs = []
    for g in range(G):
        pieces = [flat[pl.ds(base_row + 4 * s + g, mrows, stride=CHUNK_PITCH), :].astype(BF16)
                  for s in range(CMP_STRIDE)]
        xs.append(jnp.concatenate(pieces, axis=1))
    x = jnp.concatenate(xs, axis=0)
    wc = wc_ref[...]
    w2 = w2_ref[...]
    pr = jnp.dot(x, wc, preferred_element_type=F32)
    base = _cmp_base(pe_ref, wc, b1_ref)
    p0 = pr[:, :CMP_HID]
    p1 = pr[:, CMP_HID:]
    p1s = pltpu.roll(p1, G * mrows - 1, 0)
    kb = jnp.dot(jax.nn.gelu(base + p0 + p1s).astype(BF16), w2, preferred_element_type=F32)
    row0 = pl.multiple_of(j * mrows, mrows)
    for g in range(G):
        o_ref[0, g, pl.ds(row0, mrows), :] = kb[g * mrows:(g + 1) * mrows]

    zero4 = jnp.zeros((8 - G, CMP_HID), F32)

    def fix_rows(a, c):
        hfix = base + a + c
        return jnp.dot(jax.nn.gelu(hfix).astype(BF16), w2, preferred_element_type=F32)

    @pl.when(j > 0)
    def _():
        first = jnp.concatenate([p1[g * mrows:g * mrows + 1] for g in range(G)] + [zero4], axis=0)
        kf = fix_rows(carry[...], first)
        for g in range(G):
            o_ref[0, g, pl.ds(row0 - 1, 1), :] = kf[g:g + 1]

    lastp0 = jnp.concatenate([p0[(g + 1) * mrows - 1:(g + 1) * mrows] for g in range(G)] + [zero4], axis=0)
    carry[...] = lastp0

    @pl.when(j == nj - 1)
    def _():
        pn = jnp.dot(new_ref[0].astype(BF16), wc[0:NSA_HD, CMP_HID:], preferred_element_type=F32)
        kf = fix_rows(lastp0, pn)
        for g in range(G):
            o_ref[0, g, pl.ds(row0 + mrows - 1, 1), :] = kf[g:g + 1]


def _dec_blocks(page_table, pool5, layer, wcat, pe2, b1, w2, new_rows):
    nb = page_table.shape[0]
    P = PAGES_PER_STEP
    nj = N_PAGES // P
    grid_spec = pltpu.PrefetchScalarGridSpec(
        num_scalar_prefetch=1,
        grid=(nb, nj),
        in_specs=[pl.BlockSpec(memory_space=pl.ANY),
                  pl.BlockSpec((CMP_STRIDE * NSA_HD, 2 * CMP_HID), lambda b, j, pt: (0, 0)),
                  pl.BlockSpec((2, 8, CMP_STRIDE * NSA_HD), lambda b, j, pt: (0, 0, 0)),
                  pl.BlockSpec((1, CMP_HID), lambda b, j, pt: (0, 0)),
                  pl.BlockSpec((CMP_HID, NSA_HD), lambda b, j, pt: (0, 0)),
                  pl.BlockSpec((1, 8, NSA_HD), lambda b, j, pt: (b, 0, 0))],
        out_specs=pl.BlockSpec((1, NSA_KV, N_PAST_CHUNK, NSA_HD), lambda b, j, pt: (b, 0, 0, 0)),
        scratch_shapes=[pltpu.VMEM((2, 8 * P, CHUNK_PITCH, LANES), F32),
                        pltpu.SemaphoreType.DMA((2,)),
                        pltpu.VMEM((8, CMP_HID), F32)],
    )
    return pl.pallas_call(
        functools.partial(_dec_blocks_body, layer=layer),
        out_shape=jax.ShapeDtypeStruct((nb, NSA_KV, N_PAST_CHUNK, NSA_HD), F32),
        grid_spec=grid_spec,
        compiler_params=_cparams(("arbitrary", "arbitrary")),
        name="dec_blocks",
    )(page_table, pool5, wcat, pe2, b1, w2, new_rows)


def _dec_cmp_attn_body(q_ref, kb_ref, vb_ref, o_ref, idx_ref):
    G, R, hd = NSA_KV, NSA_GROUP, NSA_HD
    n_blk = N_PAST_CHUNK
    nn = lax.broadcasted_iota(jnp.int32, (n_blk, SLC_LANES), 0) * CMP_STRIDE
    jj = lax.broadcasted_iota(jnp.int32, (n_blk, SLC_LANES), 1)
    ov = ((nn < jj * SLC_LEN + SLC_LEN) & (nn + CMP_LEN > jj * SLC_LEN) & (jj < N_SLC_DEC)).astype(BF16)
    n = lax.broadcasted_iota(jnp.int32, (8, n_blk), 1)
    mask = (n * CMP_STRIDE + CMP_LEN - 1) <= PAST_LEN
    rowi = lax.broadcasted_iota(jnp.int32, (8, n_blk), 0)
    imps = []
    for g in range(G):
        q = q_ref[0, g].astype(BF16)
        s = lax.dot_general(q, kb_ref[0, g].astype(BF16), (((1,), (1,)), ((), ())),
                            preferred_element_type=F32) * (hd ** -0.5)
        s = jnp.where(mask, s, NEG)
        m = jnp.max(s, axis=-1, keepdims=True)
        e = jnp.where(mask, jnp.exp(s - m), 0.0)
        l = jnp.sum(e, axis=-1, keepdims=True)
        p = e / jnp.where(l > 0.0, l, 1.0)
        o_ref[0, g] = jnp.dot(p.astype(BF16), vb_ref[0, g].astype(BF16), preferred_element_type=F32)
        psum = jnp.sum(jnp.where(rowi < R, p, 0.0), axis=0, keepdims=True)
        imp8 = jnp.dot(jnp.broadcast_to(psum, (8, n_blk)).astype(BF16), ov, preferred_element_type=F32)
        imps.append(imp8[0:1])
    imp = jnp.concatenate(imps + [jnp.zeros((8 - G, SLC_LANES), F32)], axis=0)
    j = lax.broadcasted_iota(jnp.int32, (8, SLC_LANES), 1)
    cur = PAST_LEN // SLC_LEN
    forced = (j == 0) | (j == cur) | (j == cur - 1)
    imp = jnp.where(forced, FORCE_SCORE, jnp.where(j <= cur, imp, -1.0))
    imp = jnp.where(j < N_SLC_DEC, imp, -jnp.inf)
    lane = lax.broadcasted_iota(jnp.int32, (8, LANES), 1)
    jf = j.astype(F32)
    out = jnp.zeros((8, LANES), F32)
    for k in range(SLC_TOPK):
        mx = jnp.max(imp, axis=-1, keepdims=True)
        pick = jnp.min(jnp.where(imp == mx, jf, float(SLC_LANES)), axis=-1, keepdims=True)
        out = jnp.where(lane == k, pick, out)
        imp = jnp.where(jf == pick, -jnp.inf, imp)
    idx_ref[0] = out.astype(jnp.int32)


def _dec_cmp_attn(q, kb, vb):
    nb = q.shape[0]
    blk = pl.BlockSpec((1, NSA_KV, N_PAST_CHUNK, NSA_HD), lambda b: (b, 0, 0, 0))
    qspec = pl.BlockSpec((1, NSA_KV, 8, NSA_HD), lambda b: (b, 0, 0, 0))
    return pl.pallas_call(
        _dec_cmp_attn_body,
        out_shape=(jax.ShapeDtypeStruct((nb, NSA_KV, 8, NSA_HD), F32),
                   jax.ShapeDtypeStruct((nb, 8, LANES), jnp.int32)),
        grid=(nb,),
        in_specs=[qspec, blk, blk],
        out_specs=(qspec, pl.BlockSpec((1, 8, LANES), lambda b: (b, 0, 0))),
        compiler_params=_cparams(("parallel",)),
        name="dec_cmp_attn",
    )(q, kb, vb)


def _dec_slc_body(idx_ref, pt_ref, q_ref, kn_ref, vn_ref, pk_ref, pv_ref, o_ref, kbuf, vbuf, sem, *, layer):
    G, R, hd = NSA_KV, NSA_GROUP, NSA_HD
    K = SLC_TOPK
    b = pl.program_id(0)

    def copies(g, k):
        jsel = idx_ref[(b * G + g) * K + k]
        jp = jnp.minimum(jsel, N_PAST_BLK - 1)
        phys = pt_ref[b, lax.shift_right_logical(jp, 1)]
        rows = pl.ds((jp & 1) * SLC_LEN, SLC_LEN)
        dst = pl.ds((g * K + k) * SLC_LEN, SLC_LEN)
        ck = pltpu.make_async_copy(pk_ref.at[layer, phys, rows, g, :], kbuf.at[dst, :], sem.at[0])
        cv = pltpu.make_async_copy(pv_ref.at[layer, phys, rows, g, :], vbuf.at[dst, :], sem.at[1])
        return ck, cv

    for g in range(G):
        for k in range(K):
            ck, cv = copies(g, k)
            ck.start()
            cv.start()
    for g in range(G):
        for k in range(K):
            ck, cv = copies(g, k)
            ck.wait()
            cv.wait()

    nkeys = K * SLC_LEN
    slot = lax.broadcasted_iota(jnp.int32, (8, nkeys), 1) // SLC_LEN
    scale = hd ** -0.5
    for g in range(G):
        valid = jnp.zeros((8, nkeys), jnp.int32)
        n_new_i = jnp.int32(0)
        for k in range(K):
            jsel = idx_ref[(b * G + g) * K + k]
            past_i = jnp.where(jsel < N_PAST_BLK, 1, 0).astype(jnp.int32)
            valid = jnp.where(slot == k, past_i, valid)
            n_new_i = n_new_i + (1 - past_i)
        mask = valid > 0
        n_new = jnp.full((8, 1), n_new_i, jnp.int32).astype(F32)
        q = q_ref[0, g].astype(BF16)
        kg = kbuf[g * nkeys:(g + 1) * nkeys, :].astype(BF16)
        vg = vbuf[g * nkeys:(g + 1) * nkeys, :].astype(BF16)
        s = lax.dot_general(q, kg, (((1,), (1,)), ((), ())), preferred_element_type=F32) * scale
        s = jnp.where(mask, s, NEG)
        kn = kn_ref[0, g][0:1].astype(BF16).astype(F32)
        vn = vn_ref[0, g][0:1].astype(BF16).astype(F32)
        has_new = n_new > 0.5
        s_new = jnp.sum(q.astype(F32) * kn, axis=-1, keepdims=True) * scale
        s_new = jnp.where(has_new, s_new, NEG)
        m = jnp.maximum(jnp.max(s, axis=-1, keepdims=True), s_new)
        e = jnp.where(mask, jnp.exp(s - m), 0.0)
        e_new = jnp.where(has_new, jnp.exp(s_new - m), 0.0) * n_new
        l = jnp.sum(e, axis=-1, keepdims=True) + e_new
        pv = jnp.dot(e.astype(BF16), vg, preferred_element_type=F32) + e_new * vn
        o_ref[0, g] = pv / l


def _dec_slc(idx_flat, page_table, q, kn, vn, pool_k, pool_v, layer):
    nb = q.shape[0]
    qspec = pl.BlockSpec((1, NSA_KV, 8, NSA_HD), lambda b, idx, pt: (b, 0, 0, 0))
    nrow = NSA_KV * SLC_TOPK * SLC_LEN
    grid_spec = pltpu.PrefetchScalarGridSpec(
        num_scalar_prefetch=2,
        grid=(nb,),
        in_specs=[qspec, qspec, qspec, pl.BlockSpec(memory_space=pl.ANY), pl.BlockSpec(memory_space=pl.ANY)],
        out_specs=qspec,
        scratch_shapes=[pltpu.VMEM((nrow, NSA_HD), F32), pltpu.VMEM((nrow, NSA_HD), F32),
                        pltpu.SemaphoreType.DMA((2,))],
    )
    return pl.pallas_call(
        functools.partial(_dec_slc_body, layer=layer),
        out_shape=jax.ShapeDtypeStruct((nb, NSA_KV, 8, NSA_HD), F32),
        grid_spec=grid_spec,
        compiler_params=_cparams(("arbitrary",)),
        name="dec_slc",
    )(idx_flat, page_table, q, kn, vn, pool_k, pool_v)


def _dec_win_body(*refs, G, R, hd, L, window, use_sink):
    if use_sink:
        q_ref, kb_ref, vb_ref, kn_ref, vn_ref, sink_ref, o_ref = refs
    else:
        q_ref, kb_ref, vb_ref, kn_ref, vn_ref, o_ref = refs
    scale = hd ** -0.5
    i = lax.broadcasted_iota(jnp.int32, (8, L), 1)
    mask = (L - i) < window
    for g in range(G):
        q = q_ref[0, g].astype(BF16)
        s = lax.dot_general(q, kb_ref[0, g].astype(BF16), (((1,), (1,)), ((), ())),
                            preferred_element_type=F32) * scale
        s = jnp.where(mask, s, NEG)
        kn = kn_ref[0, g][0:1].astype(BF16).astype(F32)
        vn = vn_ref[0, g][0:1].astype(BF16).astype(F32)
        s_new = jnp.sum(q.astype(F32) * kn, axis=-1, keepdims=True) * scale
        m = jnp.maximum(jnp.max(s, axis=-1, keepdims=True), s_new)
        if use_sink:
            sk = sink_ref[g][:, 0:1]
            m = jnp.maximum(m, sk)
        e = jnp.where(mask, jnp.exp(s - m), 0.0)
        e_new = jnp.exp(s_new - m)
        l = jnp.sum(e, axis=-1, keepdims=True) + e_new
        if use_sink:
            l = l + jnp.exp(sk - m)
        pv = jnp.dot(e.astype(BF16), vb_ref[0, g].astype(BF16), preferred_element_type=F32) + e_new * vn
        o_ref[0, g] = pv / l


def _dec_win(q, kb, vb, kn, vn, sinks, *, window):
    nb, G, _, hd = q.shape
    L = kb.shape[2]
    R = 8
    qspec = pl.BlockSpec((1, G, 8, hd), lambda b: (b, 0, 0, 0))
    bspec = pl.BlockSpec((1, G, L, hd), lambda b: (b, 0, 0, 0))
    in_specs = [qspec, bspec, bspec, qspec, qspec]
    args = [q, kb, vb, kn, vn]
    if sinks is not None:
        in_specs.append(pl.BlockSpec((G, 8, LANES), lambda b: (0, 0, 0)))
        args.append(sinks)
    return pl.pallas_call(
        functools.partial(_dec_win_body, G=G, R=R, hd=hd, L=L, window=window, use_sink=sinks is not None),
        out_shape=jax.ShapeDtypeStruct((nb, G, 8, hd), F32),
        grid=(nb,),
        in_specs=in_specs,
        out_specs=qspec,
        compiler_params=_cparams(("parallel",)),
        name="dec_win",
    )(*args)


def _cmp_params(pe, w1, b1, w2):
    wcat = jnp.transpose(w1.reshape(2, CMP_STRIDE * NSA_HD, CMP_HID), (1, 0, 2)).reshape(CMP_STRIDE * NSA_HD, 2 * CMP_HID)
    pe2 = jnp.broadcast_to(pe.reshape(2, 1, CMP_STRIDE * NSA_HD), (2, 8, CMP_STRIDE * NSA_HD))
    return wcat.astype(BF16), pe2.astype(BF16), b1.reshape(1, CMP_HID), w2.astype(BF16)


def _group_rows(x, G, R, hd):
    nb = x.shape[0]
    x = x.reshape(nb, G, R, hd)
    if R < 8:
        x = jnp.pad(x, ((0, 0), (0, 0), (0, 8 - R), (0, 0)))
    return x


def _nsa_layer(a, xp, xs, caches, page_table, win_k, win_v, w_main, w_gate, w_o, cmpk, cmpv, ln_g, ln_b, tabs_p, tabs_s):
    pool_ck, pool_cv, pool_sk, pool_sv = caches
    nb = DEC_BATCH
    half = NSA_ROT // 2
    xb = xp.astype(BF16)
    h = _mm(xb, w_main, tm=1024, tn=512)
    hg = _mm(xb, w_gate, tm=1024, tn=LANES)
    wcat = jnp.stack([cmpk[0], cmpv[0]])
    pe2 = jnp.stack([cmpk[1], cmpv[1]])
    b1 = jnp.stack([cmpk[2], cmpv[2]])
    w2 = jnp.stack([cmpk[3], cmpv[3]])
    kvb = _cmp_mlp_prompt(h, wcat, pe2, b1, w2)
    o_cmp, sel = _cmp_attn_prompt(h, kvb, tq=256)
    q_rot = _rope(h, 0, NSA_Q, tabs_p, half=half, tr=512, rows_per_seq=SEQ)
    ks_rot = _rope(h, NSA_Q + 2 * NSA_KVD, NSA_KVD, tabs_p, half=half, tr=512, rows_per_seq=SEQ)
    kw_rot = _rope(h, NSA_Q + 4 * NSA_KVD, NSA_KVD, tabs_p, half=half, tr=512, rows_per_seq=SEQ)
    vs_cb = (NSA_Q + 3 * NSA_KVD) // NSA_HD
    vw_cb = (NSA_Q + 5 * NSA_KVD) // NSA_HD
    o_slc = _attn_tok(q_rot, 0, ks_rot, 0, h, vs_cb, sel, window=None, tq=256, tk=256)
    o_win = _attn_tok(q_rot, 0, kw_rot, 0, h, vw_cb, None, window=NSA_WINDOW, tq=256, tk=256)
    yp = _out_ln([o_cmp, o_slc, o_win], hg, w_o, xp, ln_g, ln_b, tm=256, tk=NSA_Q)

    def kv5(arr):
        return arr.reshape(BATCH, SEQ, NSA_KV, NSA_HD)

    def hs(i):
        return h[:, NSA_Q + i * NSA_KVD:NSA_Q + (i + 1) * NSA_KVD]

    lw = min(NSA_WINDOW, SEQ)
    st_p = (kv5(hs(0)), kv5(hs(1)), kv5(ks_rot), kv5(hs(3)), kv5(kw_rot)[:, SEQ - lw:], kv5(hs(5))[:, SEQ - lw:])

    xsb = xs.astype(BF16)
    hd_ = _mm(xsb, w_main, tm=nb, tn=512)
    hgd = _mm(xsb, w_gate, tm=nb, tn=LANES)
    qd_rot = _rope(hd_, 0, NSA_Q, tabs_s, half=half, tr=nb, rows_per_seq=nb)
    ksd_rot = _rope(hd_, NSA_Q + 2 * NSA_KVD, NSA_KVD, tabs_s, half=half, tr=nb, rows_per_seq=nb)
    kwd_rot = _rope(hd_, NSA_Q + 4 * NSA_KVD, NSA_KVD, tabs_s, half=half, tr=nb, rows_per_seq=nb)

    def hsd(i):
        return hd_[:, NSA_Q + i * NSA_KVD:NSA_Q + (i + 1) * NSA_KVD]

    kc_new, vc_new, vs_new, vw_new = hsd(0), hsd(1), hsd(3), hsd(5)
    def rows8(x):
        return jnp.pad(x.reshape(nb, NSA_KV, NSA_HD), ((0, 0), (0, 8 - NSA_KV), (0, 0)))

    kblk = _dec_blocks(page_table, pool_ck, a, cmpk[0], cmpk[1], cmpk[2], cmpk[3], rows8(kc_new))
    vblk = _dec_blocks(page_table, pool_cv, a, cmpv[0], cmpv[1], cmpv[2], cmpv[3], rows8(vc_new))
    qd = _group_rows(hd_[:, :NSA_Q], NSA_KV, NSA_GROUP, NSA_HD)
    qd_r = _group_rows(qd_rot, NSA_KV, NSA_GROUP, NSA_HD)
    od_cmp, idx = _dec_cmp_attn(qd, kblk, vblk)
    idx_flat = idx[:, :NSA_KV, :SLC_TOPK].reshape(-1)
    od_slc = _dec_slc(idx_flat, page_table, qd_r, _group_rows(ksd_rot, NSA_KV, 1, NSA_HD),
                      _group_rows(vs_new, NSA_KV, 1, NSA_HD), pool_sk, pool_sv, a)
    wk_hm = jnp.transpose(win_k, (0, 2, 1, 3))
    wv_hm = jnp.transpose(win_v, (0, 2, 1, 3))
    od_win = _dec_win(qd_r, wk_hm, wv_hm, _group_rows(kwd_rot, NSA_KV, 1, NSA_HD),
                      _group_rows(vw_new, NSA_KV, 1, NSA_HD), None, window=NSA_WINDOW)

    def flat_heads(o):
        return o[:, :, :NSA_GROUP, :].reshape(nb, NSA_Q)

    ys = _out_ln([flat_heads(od_cmp), flat_heads(od_slc), flat_heads(od_win)], hgd, w_o, xs, ln_g, ln_b, tm=nb, tk=NSA_Q)

    def new5(arr):
        return arr.reshape(nb, 1, NSA_KV, NSA_HD)

    st_s = (new5(kc_new), new5(vc_new), new5(ksd_rot), new5(vs_new),
            jnp.concatenate([win_k[:, 1:], new5(kwd_rot)], axis=1),
            jnp.concatenate([win_v[:, 1:], new5(vw_new)], axis=1))
    return yp, ys, st_p, st_s


def _swa_layer(xp, xs, buf_k, buf_v, w_in, b_in, sinks, w_o, ln_g, ln_b, tabs_p, tabs_s):
    nb = DEC_BATCH
    half = SWA_ROT // 2
    G, R, hd = SWA_KV, SWA_GROUP, SWA_HD
    xb = xp.astype(BF16)
    h = _mm(xb, w_in, b_in, tm=1024, tn=512)
    q_rot = _rope(h, 0, SWA_Q, tabs_p, half=half, tr=512, rows_per_seq=SEQ)
    k_rot = _rope(h, SWA_Q, SWA_KVD, tabs_p, half=half, tr=512, rows_per_seq=SEQ)
    v = h[:, SWA_Q + SWA_KVD:]
    q_hm = jnp.transpose(q_rot.reshape(BATCH, SEQ, G, R, hd), (0, 2, 3, 1, 4)).astype(BF16)
    k_hm = jnp.transpose(k_rot.reshape(BATCH, SEQ, G, hd), (0, 2, 1, 3)).astype(BF16)
    v_hm = jnp.transpose(v.reshape(BATCH, SEQ, G, hd), (0, 2, 1, 3)).astype(BF16)
    sink_p = jnp.broadcast_to(sinks.reshape(G, R, 1, 1), (G, R, 8, LANES))
    o_hm = _attn_head(q_hm, k_hm, v_hm, sink_p, window=SWA_WINDOW, tq=128, tk=128)
    o = jnp.transpose(o_hm, (0, 3, 1, 2, 4)).reshape(BATCH * SEQ, SWA_Q)
    yp = _out_ln([o], None, w_o, xp, ln_g, ln_b, tm=512, tk=SWA_Q)
    lw = min(SWA_WINDOW, SEQ)
    st_p = (k_rot.reshape(BATCH, SEQ, G, hd)[:, SEQ - lw:], v.reshape(BATCH, SEQ, G, hd)[:, SEQ - lw:])

    xsb = xs.astype(BF16)
    hd_ = _mm(xsb, w_in, b_in, tm=nb, tn=512)
    qd_rot = _rope(hd_, 0, SWA_Q, tabs_s, half=half, tr=nb, rows_per_seq=nb)
    kd_rot = _rope(hd_, SWA_Q, SWA_KVD, tabs_s, half=half, tr=nb, rows_per_seq=nb)
    vd = hd_[:, SWA_Q + SWA_KVD:]
    sink_s = jnp.broadcast_to(sinks.reshape(G, R, 1), (G, R, LANES))
    od = _dec_win(_group_rows(qd_rot, G, R, hd), jnp.transpose(buf_k, (0, 2, 1, 3)), jnp.transpose(buf_v, (0, 2, 1, 3)),
                  _group_rows(kd_rot, G, 1, hd), _group_rows(vd, G, 1, hd), sink_s, window=SWA_WINDOW)
    ys = _out_ln([od.reshape(nb, SWA_Q)], None, w_o, xs, ln_g, ln_b, tm=nb, tk=SWA_Q)
    kn5 = kd_rot.reshape(nb, 1, G, hd)
    vn5 = vd.reshape(nb, 1, G, hd)
    st_s = (jnp.concatenate([buf_k[:, 1:], kn5], axis=1), jnp.concatenate([buf_v[:, 1:], vn5], axis=1))
    return yp, ys, st_p, st_s


def _ffn_layer(xp, xs, state, w_up, conv_w, conv_b, w_down, ln_g, ln_b):
    nb = DEC_BATCH
    yp, sa, sg = _ffn_prompt(xp.astype(BF16), xp, w_up, conv_w, conv_b, w_down, ln_g, ln_b)
    tps = sa.shape[0] // BATCH
    conv_p = jnp.concatenate([sa[tps - 1::tps, 6:8], sg[tps - 1::tps, 6:8]], axis=-1)
    hup = _mm(xs.astype(BF16), w_up, tm=nb, tn=512)
    gated = _dec_gate(hup, state[:, 0], state[:, 1], conv_w, conv_b)
    ys = _out_ln([gated], None, w_down, xs, ln_g, ln_b, tm=nb, tk=512)
    conv_s = jnp.concatenate([state[:, 1:], hup[:, None, :]], axis=1)
    return yp, ys, conv_p, conv_s


def kernel(x_prompt, x_sample, cache_nsa_cmp_k, cache_nsa_cmp_v, cache_nsa_slc_k, cache_nsa_slc_v, state_nsa_win_k, state_nsa_win_v, state_swa_k, state_swa_v, state_conv, page_table, nsa_w_in, nsa_w_o, nsa_cmp_pe_k, nsa_cmp_w1_k, nsa_cmp_b1_k, nsa_cmp_w2_k, nsa_cmp_pe_v, nsa_cmp_w1_v, nsa_cmp_b1_v, nsa_cmp_w2_v, swa_w_in, swa_b_in, swa_sinks, swa_w_o, ln1_g, ln1_b, ln2_g, ln2_b, ffn_w_up, ffn_conv_w, ffn_conv_b, ffn_w_down):
    n_nsa, n_pool = cache_nsa_cmp_k.shape[:2]
    pool_ck = cache_nsa_cmp_k.reshape(n_nsa, n_pool, 8, 64, NSA_HD)
    pool_cv = cache_nsa_cmp_v.reshape(n_nsa, n_pool, 8, 64, NSA_HD)
    pool_sk = cache_nsa_slc_k
    pool_sv = cache_nsa_slc_v

    pos_p = jnp.arange(SEQ)
    pos_s = jnp.full((DEC_BATCH,), PAST_LEN)
    nsa_tabs_p = _rope_tables(pos_p, NSA_ROT, NSA_HD)
    nsa_tabs_s = _rope_tables(pos_s, NSA_ROT, NSA_HD)
    swa_tabs_p = _rope_tables(pos_p, SWA_ROT, SWA_HD)
    swa_tabs_s = _rope_tables(pos_s, SWA_ROT, SWA_HD)

    yp = x_prompt.reshape(BATCH * SEQ, D_MODEL)
    ys = x_sample.reshape(DEC_BATCH, D_MODEL)
    nsa_p, nsa_s, swa_p, swa_s, conv_p, conv_s = [], [], [], [], [], []
    for i in range(DEPTH):
        a = i // 2
        g1, b1 = ln1_g[i].reshape(1, D_MODEL), ln1_b[i].reshape(1, D_MODEL)
        g2, b2 = ln2_g[i].reshape(1, D_MODEL), ln2_b[i].reshape(1, D_MODEL)
        if i % 2 == 0:
            w_in = nsa_w_in[a]
            w_main = w_in[:, :NSA_MAIN].astype(BF16)
            w_gate = jnp.pad(w_in[:, NSA_MAIN:], ((0, 0), (0, LANES - 3 * NSA_HEADS))).astype(BF16)
            cmpk = _cmp_params(nsa_cmp_pe_k[a], nsa_cmp_w1_k[a], nsa_cmp_b1_k[a], nsa_cmp_w2_k[a])
            cmpv = _cmp_params(nsa_cmp_pe_v[a], nsa_cmp_w1_v[a], nsa_cmp_b1_v[a], nsa_cmp_w2_v[a])
            yp, ys, st_p, st_s = _nsa_layer(
                a, yp, ys, (pool_ck, pool_cv, pool_sk, pool_sv), page_table, state_nsa_win_k[a], state_nsa_win_v[a],
                w_main, w_gate, nsa_w_o[a].astype(BF16), cmpk, cmpv, g1, b1, nsa_tabs_p, nsa_tabs_s)
            nsa_p.append(st_p)
            nsa_s.append(st_s)
        else:
            yp, ys, st_p, st_s = _swa_layer(
                yp, ys, state_swa_k[a], state_swa_v[a], swa_w_in[a].astype(BF16), swa_b_in[a].reshape(1, -1),
                swa_sinks[a], swa_w_o[a].astype(BF16), g1, b1, swa_tabs_p, swa_tabs_s)
            swa_p.append(st_p)
            swa_s.append(st_s)
        yp, ys, cp, cs = _ffn_layer(yp, ys, state_conv[i], ffn_w_up[i].astype(BF16), ffn_conv_w[i],
                                    ffn_conv_b[i].reshape(1, -1), ffn_w_down[i].astype(BF16), g2, b2)
        conv_p.append(cp)
        conv_s.append(cs)

    p_cmp_k, p_cmp_v, p_slc_k, p_slc_v, p_nwin_k, p_nwin_v = [jnp.stack(t) for t in zip(*nsa_p)]
    s_cmp_k, s_cmp_v, s_slc_k, s_slc_v, s_nwin_k, s_nwin_v = [jnp.stack(t) for t in zip(*nsa_s)]
    p_swa_k, p_swa_v = [jnp.stack(t) for t in zip(*swa_p)]
    s_swa_k, s_swa_v = [jnp.stack(t) for t in zip(*swa_s)]
    return (yp.reshape(BATCH, SEQ, D_MODEL), ys.reshape(DEC_BATCH, 1, D_MODEL),
            p_cmp_k, p_cmp_v, p_slc_k, p_slc_v, p_nwin_k, p_nwin_v, p_swa_k, p_swa_v, jnp.stack(conv_p),
            s_cmp_k, s_cmp_v, s_slc_k, s_slc_v, s_nwin_k, s_nwin_v, s_swa_k, s_swa_v, jnp.stack(conv_s))
```

```python
import functools
import math

import jax
import jax.numpy as jnp
from jax import lax
from jax.experimental import pallas as pl
from jax.experimental.pallas import tpu as pltpu

F32 = jnp.float32
BF16 = jnp.bfloat16

D_MODEL = 2048
BATCH = 4
SEQ = 2048
DEPTH = 4
DEC_BATCH = 32
PAST_LEN = 16384
PAGE_SIZE = 128

NSA_HEADS = 16
NSA_KV = 4
NSA_HD = 128
NSA_GROUP = 4
NSA_ROT = 32
NSA_Q = 2048
NSA_KVD = 512
NSA_MAIN = NSA_Q + 6 * NSA_KVD
CMP_LEN = 32
CMP_STRIDE = 16
CMP_HID = 256
SLC_LEN = 64
SLC_TOPK = 16
NSA_WINDOW = 512
FORCE_SCORE = 1e4
SWA_HEADS = 32
SWA_KV = 4
SWA_HD = 64
SWA_GROUP = 8
SWA_ROT = 16
SWA_Q = 2048
SWA_KVD = 256
SWA_WINDOW = 128
ROPE_THETA = 500000.0
D_FF = 5632
ALPHA = (2 * DEPTH) ** 0.25
LN_EPS = 1e-5
NEG = -1e30

LANES = 128
PAGES_PER_STEP = 32
CHUNK_PITCH = 72
N_PAGES = PAST_LEN // PAGE_SIZE
N_PAST_CHUNK = PAST_LEN // CMP_STRIDE
N_PAST_BLK = PAST_LEN // SLC_LEN
N_SLC_DEC = N_PAST_BLK + 1
SLC_LANES = 384
VMEM_LIMIT = 56 * 1024 * 1024


def _cparams(sem):
    return pltpu.CompilerParams(dimension_semantics=sem, vmem_limit_bytes=VMEM_LIMIT)


def _ln(z, g, b):
    mu = jnp.mean(z, axis=-1, keepdims=True)
    d = z - mu
    var = jnp.mean(d * d, axis=-1, keepdims=True)
    return d * lax.rsqrt(var + LN_EPS) * g + b


def _mm_body(*refs, has_bias):
    if has_bias:
        x_ref, w_ref, b_ref, o_ref = refs
    else:
        x_ref, w_ref, o_ref = refs
    acc = jnp.dot(x_ref[...], w_ref[...], preferred_element_type=F32)
    if has_bias:
        acc = acc + b_ref[...]
    o_ref[...] = acc


def _mm(x, w, bias=None, *, tm, tn):
    M, K = x.shape
    N = w.shape[1]
    in_specs = [pl.BlockSpec((tm, K), lambda i, j: (i, 0)),
                pl.BlockSpec((K, tn), lambda i, j: (0, j))]
    args = [x, w]
    if bias is not None:
        in_specs.append(pl.BlockSpec((1, tn), lambda i, j: (0, j)))
        args.append(bias)
    return pl.pallas_call(
        functools.partial(_mm_body, has_bias=bias is not None),
        out_shape=jax.ShapeDtypeStruct((M, N), F32),
        grid=(M // tm, N // tn),
        in_specs=in_specs,
        out_specs=pl.BlockSpec((tm, tn), lambda i, j: (i, j)),
        compiler_params=_cparams(("parallel", "arbitrary")),
        name="mm",
    )(*args)


def _out_ln_body(*refs, n_lhs, gated, nk):
    lhs_refs = refs[:n_lhs]
    pos = n_lhs
    gate_ref = None
    if gated:
        gate_ref = refs[pos]
        pos += 1
    w_ref, res_ref, g_ref, b_ref, y_ref, yb_ref, acc_ref = refs[pos:pos + 7]
    k = pl.program_id(1)
    if gated:
        sig = jax.nn.sigmoid(gate_ref[...])
        vals = [r[...] for r in lhs_refs]
        cols = []
        for h in range(NSA_HEADS):
            sl = slice(h * NSA_HD, (h + 1) * NSA_HD)
            m = None
            for br in range(3):
                c = br * NSA_HEADS + h
                t = sig[:, c:c + 1] * vals[br][:, sl]
                m = t if m is None else m + t
            cols.append(m)
        lhs = jnp.concatenate(cols, axis=1).astype(BF16)
    else:
        lhs = lhs_refs[0][...].astype(BF16)
    contrib = jnp.dot(lhs, w_ref[...], preferred_element_type=F32)

    @pl.when(k == 0)
    def _():
        acc_ref[...] = contrib

    @pl.when(k > 0)
    def _():
        acc_ref[...] += contrib

    @pl.when(k == nk - 1)
    def _():
        y = _ln(ALPHA * res_ref[...] + acc_ref[...], g_ref[...], b_ref[...])
        y_ref[...] = y
        yb_ref[...] = y.astype(BF16)


def _out_ln(lhs_list, gate, w, res, g, b, *, tm, tk):
    M, K = lhs_list[0].shape
    N = w.shape[1]
    nk = K // tk
    gated = gate is not None
    assert not gated or nk == 1
    in_specs = [pl.BlockSpec((tm, tk), lambda i, k: (i, k)) for _ in lhs_list]
    args = list(lhs_list)
    if gated:
        in_specs.append(pl.BlockSpec((tm, LANES), lambda i, k: (i, 0)))
        args.append(gate)
    in_specs += [pl.BlockSpec((tk, N), lambda i, k: (k, 0)),
                 pl.BlockSpec((tm, N), lambda i, k: (i, 0)),
                 pl.BlockSpec((1, N), lambda i, k: (0, 0)),
                 pl.BlockSpec((1, N), lambda i, k: (0, 0))]
    args += [w, res, g, b]
    return pl.pallas_call(
        functools.partial(_out_ln_body, n_lhs=len(lhs_list), gated=gated, nk=nk),
        out_shape=(jax.ShapeDtypeStruct((M, N), F32), jax.ShapeDtypeStruct((M, N), BF16)),
        grid=(M // tm, nk),
        in_specs=in_specs,
        out_specs=(pl.BlockSpec((tm, N), lambda i, k: (i, 0)), pl.BlockSpec((tm, N), lambda i, k: (i, 0))),
        scratch_shapes=[pltpu.VMEM((tm, N), F32)],
        compiler_params=_cparams(("parallel", "arbitrary")),
        name="out_ln",
    )(*args)


def _ffn_body(xb_ref, xr_ref, wa_ref, wg_ref, cwa_ref, cwg_ref, cba_ref, cbg_ref, wd_ref, g_ref, b_ref,
              y_ref, yb_ref, sa_ref, sg_ref, acc_ref, carry_a, carry_g, *, tiles_per_seq, nj, tm, tc, cw_split):
    i = pl.program_id(0)
    j = pl.program_id(1)
    first = (i % tiles_per_seq) == 0
    x = xb_ref[...]

    @pl.when(first)
    def _():
        carry_a[j] = jnp.zeros(carry_a.shape[1:], F32)
        carry_g[j] = jnp.zeros(carry_g.shape[1:], F32)

    row = lax.broadcasted_iota(jnp.int32, (8, cw_split), 0)

    def branch(cs, w_ref, cw_ref, cb_ref, carry, st_ref):
        h = jnp.dot(x, w_ref[:, cs], preferred_element_type=F32)
        prev = carry[j, :, cs]
        r1 = pltpu.roll(h, 1, 0)
        r2 = pltpu.roll(h, 2, 0)
        h1 = jnp.concatenate([jnp.where(row == 0, prev[7:8], r1[0:8]), r1[8:]], axis=0)
        h2 = jnp.concatenate([jnp.where(row == 0, prev[6:7], jnp.where(row == 1, prev[7:8], r2[0:8])),
                              r2[8:]], axis=0)
        cw = cw_ref[:, cs]
        c = cb_ref[:, cs] + h2 * cw[0:1] + h1 * cw[1:2] + h * cw[2:3]
        last = h[tm - 8:tm]
        carry[j, :, cs] = last
        st_ref[0, :, cs] = last
        return c

    gated = []
    for c0 in range(0, tc, cw_split):
        cs = slice(c0, c0 + cw_split)
        ca = branch(cs, wa_ref, cwa_ref, cba_ref, carry_a, sa_ref)
        cg = branch(cs, wg_ref, cwg_ref, cbg_ref, carry_g, sg_ref)
        gated.append((ca * jax.nn.gelu(cg)).astype(BF16))
    contrib = jnp.dot(jnp.concatenate(gated, axis=1), wd_ref[...], preferred_element_type=F32)

    @pl.when(j == 0)
    def _():
        acc_ref[...] = contrib

    @pl.when(j > 0)
    def _():
        acc_ref[...] += contrib

    @pl.when(j == nj - 1)
    def _():
        y = _ln(ALPHA * xr_ref[...] + acc_ref[...], g_ref[...], b_ref[...])
        y_ref[...] = y
        yb_ref[...] = y.astype(BF16)


def _ffn_prompt(xb, xr, w_up, conv_w, conv_b, w_down, g, b, *, tm=512, tc=512):
    M = xb.shape[0]
    nj = D_FF // tc
    nb = M // tm
    tiles_per_seq = SEQ // tm
    y, yb, sa, sg = pl.pallas_call(
        functools.partial(_ffn_body, tiles_per_seq=tiles_per_seq, nj=nj, tm=tm, tc=tc, cw_split=256),
        out_shape=(jax.ShapeDtypeStruct((M, D_MODEL), F32),
                   jax.ShapeDtypeStruct((M, D_MODEL), BF16),
                   jax.ShapeDtypeStruct((nb, 8, D_FF), F32),
                   jax.ShapeDtypeStruct((nb, 8, D_FF), F32)),
        grid=(M // tm, nj),
        in_specs=[pl.BlockSpec((tm, D_MODEL), lambda i, j: (i, 0)),
                  pl.BlockSpec((tm, D_MODEL), lambda i, j: (i, 0)),
                  pl.BlockSpec((D_MODEL, tc), lambda i, j: (0, j)),
                  pl.BlockSpec((D_MODEL, tc), lambda i, j: (0, j + nj)),
                  pl.BlockSpec((3, tc), lambda i, j: (0, j)),
                  pl.BlockSpec((3, tc), lambda i, j: (0, j + nj)),
                  pl.BlockSpec((1, tc), lambda i, j: (0, j)),
                  pl.BlockSpec((1, tc), lambda i, j: (0, j + nj)),
                  pl.BlockSpec((tc, D_MODEL), lambda i, j: (j, 0)),
                  pl.BlockSpec((1, D_MODEL), lambda i, j: (0, 0)),
                  pl.BlockSpec((1, D_MODEL), lambda i, j: (0, 0))],
        out_specs=(pl.BlockSpec((tm, D_MODEL), lambda i, j: (i, 0)),
                   pl.BlockSpec((tm, D_MODEL), lambda i, j: (i, 0)),
                   pl.BlockSpec((1, 8, tc), lambda i, j: (i, 0, j)),
                   pl.BlockSpec((1, 8, tc), lambda i, j: (i, 0, j))),
        scratch_shapes=[pltpu.VMEM((tm, D_MODEL), F32),
                        pltpu.VMEM((nj, 8, tc), F32),
                        pltpu.VMEM((nj, 8, tc), F32)],
        compiler_params=_cparams(("arbitrary", "arbitrary")),
        name="ffn",
    )(xb, xr, w_up, w_up, conv_w, conv_w, conv_b, conv_b, w_down, g, b)
    return y, yb, sa, sg


def _dec_gate_body(ha_ref, hg_ref, s0a_ref, s0g_ref, s1a_ref, s1g_ref, cwa_ref, cwg_ref, cba_ref, cbg_ref, o_ref):
    def conv(h_ref, s0_ref, s1_ref, cw_ref, cb_ref):
        cw = cw_ref[...]
        return cb_ref[...] + s0_ref[...] * cw[0:1] + s1_ref[...] * cw[1:2] + h_ref[...] * cw[2:3]

    ca = conv(ha_ref, s0a_ref, s1a_ref, cwa_ref, cba_ref)
    cg = conv(hg_ref, s0g_ref, s1g_ref, cwg_ref, cbg_ref)
    o_ref[...] = (ca * jax.nn.gelu(cg)).astype(BF16)


def _dec_gate(h, s0, s1, conv_w, conv_b, *, tc=512):
    nb = h.shape[0]
    nj = D_FF // tc
    a_spec = pl.BlockSpec((nb, tc), lambda j: (0, j))
    g_spec = pl.BlockSpec((nb, tc), lambda j: (0, j + nj))
    return pl.pallas_call(
        _dec_gate_body,
        out_shape=jax.ShapeDtypeStruct((nb, D_FF), BF16),
        grid=(nj,),
        in_specs=[a_spec, g_spec, a_spec, g_spec, a_spec, g_spec,
                  pl.BlockSpec((3, tc), lambda j: (0, j)), pl.BlockSpec((3, tc), lambda j: (0, j + nj)),
                  pl.BlockSpec((1, tc), lambda j: (0, j)), pl.BlockSpec((1, tc), lambda j: (0, j + nj))],
        out_specs=pl.BlockSpec((nb, tc), lambda j: (0, j)),
        compiler_params=_cparams(("arbitrary",)),
        name="dec_gate",
    )(h, h, s0, s0, s1, s1, conv_w, conv_w, conv_b, conv_b)


def _rope_tables(pos, rot, hd):
    half = rot // 2
    inv = ROPE_THETA ** (-jnp.arange(half, dtype=F32) / half)
    ang = pos.astype(F32)[:, None] * inv[None, :]
    cos, sin = jnp.cos(ang), jnp.sin(ang)
    n = pos.shape[0]
    ones = jnp.ones((n, hd - rot), F32)
    zeros_r = jnp.zeros((n, hd - rot), F32)
    zeros_h = jnp.zeros((n, half), F32)
    c = jnp.concatenate([cos, cos, ones], axis=1)
    s1 = jnp.concatenate([zeros_h, sin, zeros_r], axis=1)
    s2 = jnp.concatenate([-sin, zeros_h, zeros_r], axis=1)
    reps = LANES // hd
    return tuple(jnp.tile(t, (1, reps)) for t in (c, s1, s2))


def _rope_body(x_ref, c_ref, s1_ref, s2_ref, o_ref, *, half):
    x = x_ref[...]
    w = x.shape[1]
    reps = w // LANES
    c = jnp.concatenate([c_ref[...]] * reps, axis=1)
    s1 = jnp.concatenate([s1_ref[...]] * reps, axis=1)
    s2 = jnp.concatenate([s2_ref[...]] * reps, axis=1)
    o_ref[...] = x * c + pltpu.roll(x, half, 1) * s1 + pltpu.roll(x, w - half, 1) * s2


def _rope(h, col0, width, tabs, *, half, tr, rows_per_seq):
    M = h.shape[0]
    cb = col0 // width
    nt = rows_per_seq // tr
    tab_spec = pl.BlockSpec((tr, LANES), lambda i: (i % nt, 0))
    return pl.pallas_call(
        functools.partial(_rope_body, half=half),
        out_shape=jax.ShapeDtypeStruct((M, width), F32),
        grid=(M // tr,),
        in_specs=[pl.BlockSpec((tr, width), lambda i: (i, cb)), tab_spec, tab_spec, tab_spec],
        out_specs=pl.BlockSpec((tr, width), lambda i: (i, 0)),
        compiler_params=_cparams(("parallel",)),
        name="rope",
    )(h, *tabs)


def _band_bias(tq, window, seq):
    w_al = -(-window // tq) * tq
    span = w_al + tq
    n_var = w_al // tq + 1
    r = jnp.arange(tq)[:, None]
    c = jnp.arange(span)[None, :]
    tabs = []
    for v in range(n_var):
        t0 = v * tq
        qpos = t0 + r
        kpos = max(t0 - w_al, 0) + c
        ok = (kpos <= qpos) & (qpos - kpos < window)
        tabs.append(jnp.where(ok, 0.0, NEG).astype(F32))
    return jnp.stack(tabs), w_al, span, n_var


def _stack_heads(qv, R, hd):
    return jnp.concatenate([qv[:, r * hd:(r + 1) * hd] for r in range(R)], axis=0)


def _unstack_heads(o, R, tq):
    return jnp.concatenate([o[r * tq:(r + 1) * tq] for r in range(R)], axis=1)


def _lane_max(s):
    m = s[:, :LANES]
    for c in range(1, s.shape[1] // LANES):
        m = jnp.maximum(m, s[:, c * LANES:(c + 1) * LANES])
    return m


def _row_max_lanes(mpart):
    return jnp.broadcast_to(jnp.max(mpart, axis=-1, keepdims=True), mpart.shape)


def _with_ones(v):
    return jnp.concatenate([v.astype(BF16), jnp.ones(v.shape, BF16)], axis=1)


def _win_body(q_ref, k_ref, v_ref, bias_ref, o_ref, *, tq, w_al, span):
    R, hd = NSA_GROUP, NSA_HD
    t0 = pl.program_id(2) * tq
    start = pl.multiple_of(jnp.maximum(t0 - w_al, 0), tq)
    q = _stack_heads(q_ref[...] * (hd ** -0.5), R, hd).astype(BF16)
    kb = k_ref[pl.ds(start, span), :].astype(BF16)
    vb = _with_ones(v_ref[pl.ds(start, span), :])
    s = lax.dot_general(q, kb, (((1,), (1,)), ((), ())), preferred_element_type=F32)
    s = (s.reshape(R, tq, span) + bias_ref[0][None]).reshape(R * tq, span)
    mb = _row_max_lanes(_lane_max(s))
    p = jnp.exp(s - jnp.concatenate([mb] * (span // LANES), axis=1))
    acc = jnp.dot(p.astype(BF16), vb, preferred_element_type=F32)
    o_ref[...] = _unstack_heads(acc[:, :hd] / acc[:, hd:], R, tq)


def _attn_win(q, k, v, v_cb, *, tq):
    R, hd, G = NSA_GROUP, NSA_HD, NSA_KV
    M = q.shape[0]
    nb = M // SEQ
    nq = SEQ // tq
    bias, w_al, span, n_var = _band_bias(tq, NSA_WINDOW, SEQ)
    return pl.pallas_call(
        functools.partial(_win_body, tq=tq, w_al=w_al, span=span),
        out_shape=jax.ShapeDtypeStruct((M, NSA_Q), F32),
        grid=(nb, G, nq),
        in_specs=[pl.BlockSpec((tq, R * hd), lambda b, g, i: (b * nq + i, g)),
                  pl.BlockSpec((SEQ, hd), lambda b, g, i: (b, g)),
                  pl.BlockSpec((SEQ, hd), lambda b, g, i: (b, v_cb + g)),
                  pl.BlockSpec((1, tq, span), lambda b, g, i: (jnp.minimum(i, n_var - 1), 0, 0))],
        out_specs=pl.BlockSpec((tq, R * hd), lambda b, g, i: (b * nq + i, g)),
        compiler_params=_cparams(("parallel", "parallel", "arbitrary")),
        name="attn_win",
    )(q, k, v, bias)


def _slc_body(q_ref, sel_ref, k_ref, v_ref, e_ref, tri_ref, o_ref, m_sc, acc_sc, *, tq):
    R, hd = NSA_GROUP, NSA_HD
    i = pl.program_id(2)
    t0 = pl.multiple_of(i * tq, tq)
    q = _stack_heads(q_ref[...] * (hd ** -0.5), R, hd)
    selm1 = sel_ref[0, 0] - 1.0
    qa = jnp.concatenate([q, jnp.concatenate([selm1] * R, axis=0)], axis=1).astype(BF16)

    def scores(k0):
        kb = jnp.concatenate([k_ref[pl.ds(k0, tq), :].astype(BF16), e_ref[pl.ds(k0, tq), :]], axis=1)
        return lax.dot_general(qa, kb, (((1,), (1,)), ((), ())), preferred_element_type=F32)

    def diag_scores():
        return (scores(t0).reshape(R, tq, tq) + tri_ref[...][None]).reshape(R * tq, tq)

    m_sc[...] = _lane_max(diag_scores())

    def max_step(kt, carry):
        m_sc[...] = jnp.maximum(m_sc[...], _lane_max(scores(pl.multiple_of(kt * tq, tq))))
        return carry

    lax.fori_loop(0, i, max_step, 0)
    mb = _row_max_lanes(m_sc[...])
    m_sc[...] = mb

    def weighted(s, k0):
        m2 = m_sc[...]
        p = jnp.exp(s - jnp.concatenate([m2] * (tq // LANES), axis=1))
        return jnp.dot(p.astype(BF16), _with_ones(v_ref[pl.ds(k0, tq), :]), preferred_element_type=F32)

    acc_sc[...] = weighted(diag_scores(), t0)

    def sum_step(kt, carry):
        k0 = pl.multiple_of(kt * tq, tq)
        acc_sc[...] += weighted(scores(k0), k0)
        return carry

    lax.fori_loop(0, i, sum_step, 0)
    acc = acc_sc[...]
    o_ref[...] = _unstack_heads(acc[:, :hd] / acc[:, hd:], R, tq)


def _attn_slc(q, k, v, v_cb, sel, *, tq):
    R, hd, G = NSA_GROUP, NSA_HD, NSA_KV
    M = q.shape[0]
    nb = M // SEQ
    nq = SEQ // tq
    kk = jnp.arange(SEQ)[:, None] // SLC_LEN
    e30 = jnp.where(kk == jnp.arange(LANES)[None, :], -NEG, 0.0).astype(BF16)
    r = jnp.arange(tq)
    tri = jnp.where(r[None, :] <= r[:, None], 0.0, NEG).astype(F32)
    return pl.pallas_call(
        functools.partial(_slc_body, tq=tq),
        out_shape=jax.ShapeDtypeStruct((M, NSA_Q), F32),
        grid=(nb, G, nq),
        in_specs=[pl.BlockSpec((tq, R * hd), lambda b, g, i: (b * nq + i, g)),
                  pl.BlockSpec((1, 1, tq, LANES), lambda b, g, i: (b, g, i, 0)),
                  pl.BlockSpec((SEQ, hd), lambda b, g, i: (b, g)),
                  pl.BlockSpec((SEQ, hd), lambda b, g, i: (b, v_cb + g)),
                  pl.BlockSpec((SEQ, LANES), lambda b, g, i: (0, 0)),
                  pl.BlockSpec((tq, tq), lambda b, g, i: (0, 0))],
        out_specs=pl.BlockSpec((tq, R * hd), lambda b, g, i: (b * nq + i, g)),
        scratch_shapes=[pltpu.VMEM((R * tq, LANES), F32), pltpu.VMEM((R * tq, 2 * hd), F32)],
        compiler_params=_cparams(("parallel", "parallel", "arbitrary")),
        name="attn_slc",
    )(q, sel, k, v, e30, tri)


def _swa_body(q_ref, k_ref, v_ref, bias_ref, sink_ref, o_ref, *, tq, w_al, span):
    npair = SWA_GROUP // 2
    t0 = pl.program_id(2) * tq
    start = pl.multiple_of(jnp.maximum(t0 - w_al, 0), tq)
    qv = q_ref[...] * (SWA_HD ** -0.5)
    qp = _stack_heads(qv, npair, LANES).astype(BF16)
    kf = k_ref[pl.ds(start, span), :]
    vf = v_ref[pl.ds(start, span), :]
    low_v = lax.broadcasted_iota(jnp.int32, (span, LANES), 1) < SWA_HD
    ke, ko = kf.astype(BF16), pltpu.roll(kf, SWA_HD, 1).astype(BF16)
    ve = jnp.where(low_v, vf, 1.0).astype(BF16)
    vo = jnp.where(low_v, 1.0, pltpu.roll(vf, SWA_HD, 1)).astype(BF16)
    dn = (((1,), (1,)), ((), ()))
    s = jnp.concatenate([lax.dot_general(qp, ke, dn, preferred_element_type=F32),
                         lax.dot_general(qp, ko, dn, preferred_element_type=F32)], axis=0)
    s = (s.reshape(SWA_GROUP, tq, span) + bias_ref[0][None]).reshape(SWA_GROUP * tq, span)
    sink = sink_ref[0]
    sink_b = jnp.concatenate([jnp.concatenate([sink[h]] * (tq // 8), axis=0) for h in range(SWA_GROUP)], axis=0)
    mb = jnp.maximum(_row_max_lanes(_lane_max(s)), sink_b)
    p = jnp.exp(s - jnp.concatenate([mb] * (span // LANES), axis=1)).astype(BF16)
    t_sink = jnp.exp(sink_b - mb)
    half = npair * tq
    o_e = jnp.dot(p[:half], ve, preferred_element_type=F32)
    o_o = jnp.dot(p[half:], vo, preferred_element_type=F32)
    low = lax.broadcasted_iota(jnp.int32, (half, LANES), 1) < SWA_HD
    numer = jnp.where(low, o_e, o_o)
    denom = pltpu.roll(jnp.where(low, o_o, o_e), SWA_HD, 1) + jnp.where(low, t_sink[:half], t_sink[half:])
    o_ref[...] = _unstack_heads(numer / denom, npair, tq)


def _attn_swa(q, k, v, v_cb, sinks, *, tq):
    G = SWA_KV
    M = q.shape[0]
    nb = M // SEQ
    nq = SEQ // tq
    bias, w_al, span, n_var = _band_bias(tq, SWA_WINDOW, SEQ)
    return pl.pallas_call(
        functools.partial(_swa_body, tq=tq, w_al=w_al, span=span),
        out_shape=jax.ShapeDtypeStruct((M, SWA_Q), F32),
        grid=(nb, G, nq),
        in_specs=[pl.BlockSpec((tq, SWA_GROUP * SWA_HD), lambda b, g, i: (b * nq + i, g)),
                  pl.BlockSpec((SEQ, LANES), lambda b, g, i: (b, g)),
                  pl.BlockSpec((SEQ, LANES), lambda b, g, i: (b, v_cb + g)),
                  pl.BlockSpec((1, tq, span), lambda b, g, i: (jnp.minimum(i, n_var - 1), 0, 0)),
                  pl.BlockSpec((1, SWA_GROUP, 8, LANES), lambda b, g, i: (g, 0, 0, 0))],
        out_specs=pl.BlockSpec((tq, SWA_GROUP * SWA_HD), lambda b, g, i: (b * nq + i, g)),
        compiler_params=_cparams(("parallel", "parallel", "arbitrary")),
        name="attn_swa",
    )(q, k, v, bias, sinks)


def _cmp_base(pe_ref, wc, b1_ref):
    base = (jnp.dot(pe_ref[0], wc[:, :CMP_HID], preferred_element_type=F32)
            + jnp.dot(pe_ref[1], wc[:, CMP_HID:], preferred_element_type=F32))
    return base[0:1] + b1_ref[...]


def _cmp_mlp_body(x_ref, wc_ref, pe_ref, b1_ref, w2_ref, o_ref, *, n_chunk):
    pieces = [x_ref[pl.ds(s, n_chunk, stride=CMP_STRIDE), :].astype(BF16) for s in range(CMP_STRIDE)]
    x = jnp.concatenate(pieces, axis=1)
    wc = wc_ref[0]
    p = jnp.dot(x, wc, preferred_element_type=F32)
    base = _cmp_base(pe_ref.at[0], wc, b1_ref.at[0])
    p1s = pltpu.roll(p[:, CMP_HID:], n_chunk - 1, 0)
    h = base + p[:, :CMP_HID] + p1s
    kb = jnp.dot(jax.nn.gelu(h).astype(BF16), w2_ref[0], preferred_element_type=F32)
    o_ref[0, 0, 0] = kb.astype(BF16)


def _cmp_mlp_prompt(h, wcat, pe2, b1, w2):
    M = h.shape[0]
    nb = M // SEQ
    n_chunk = SEQ // CMP_STRIDE
    cb0 = NSA_Q // NSA_HD
    return pl.pallas_call(
        functools.partial(_cmp_mlp_body, n_chunk=n_chunk),
        out_shape=jax.ShapeDtypeStruct((2, nb, NSA_KV, n_chunk, NSA_HD), BF16),
        grid=(2, nb, NSA_KV),
        in_specs=[pl.BlockSpec((SEQ, NSA_HD), lambda c, b, g: (b, cb0 + 4 * c + g)),
                  pl.BlockSpec((1, CMP_STRIDE * NSA_HD, 2 * CMP_HID), lambda c, b, g: (c, 0, 0)),
                  pl.BlockSpec((1, 2, 8, CMP_STRIDE * NSA_HD), lambda c, b, g: (c, 0, 0, 0)),
                  pl.BlockSpec((1, 1, CMP_HID), lambda c, b, g: (c, 0, 0)),
                  pl.BlockSpec((1, CMP_HID, NSA_HD), lambda c, b, g: (c, 0, 0))],
        out_specs=pl.BlockSpec((1, 1, 1, n_chunk, NSA_HD), lambda c, b, g: (c, b, g, 0, 0)),
        compiler_params=_cparams(("parallel", "parallel", "parallel")),
        name="cmp_mlp",
    )(h, wcat, pe2, b1, w2)


def _cmp_attn_body(q_ref, kb_ref, vb_ref, ov_ref, o_ref, sel_ref, *, tq, n_cmp, n_slc):
    R, hd = NSA_GROUP, NSA_HD
    i = pl.program_id(2)
    t0 = i * tq
    qv = q_ref[...]
    q = jnp.concatenate([qv[:, r * hd:(r + 1) * hd] for r in range(R)], axis=0).astype(BF16)
    s = lax.dot_general(q, kb_ref[0, 0, 0], (((1,), (1,)), ((), ())), preferred_element_type=F32) * (hd ** -0.5)
    n = lax.broadcasted_iota(jnp.int32, (tq, LANES), 1)
    qpos = t0 + lax.broadcasted_iota(jnp.int32, (tq, LANES), 0)
    mask = ((n * CMP_STRIDE + CMP_LEN - 1) <= qpos) & (n < n_cmp)
    mask3 = mask[None]
    s3 = jnp.where(mask3, s.reshape(R, tq, LANES), NEG)
    m = jnp.max(s3, axis=-1, keepdims=True)
    e = jnp.where(mask3, jnp.exp(s3 - m), 0.0)
    l = jnp.sum(e, axis=-1, keepdims=True)
    p = e / jnp.where(l > 0.0, l, 1.0)
    o = jnp.dot(p.reshape(R * tq, LANES).astype(BF16), vb_ref[0, 0, 0], preferred_element_type=F32)
    o_ref[...] = jnp.concatenate([o[r * tq:(r + 1) * tq] for r in range(R)], axis=1)

    psum = jnp.sum(p, axis=0)
    imp = jnp.dot(psum.astype(BF16), ov_ref[...], preferred_element_type=F32)
    j = n
    cur = qpos // SLC_LEN
    forced = (j == 0) | (j == cur) | (j == cur - 1)
    imp = jnp.where(forced, FORCE_SCORE, jnp.where(j <= cur, imp, -1.0))
    imp = jnp.where(j < n_slc, imp, -jnp.inf)
    cnt = jnp.zeros((tq, LANES), F32)
    for c in range(n_slc):
        ci = imp[:, c:c + 1]
        before = (ci > imp) | ((ci == imp) & (j > c))
        cnt = cnt + before.astype(F32)
    sel_ref[0, 0] = ((cnt < float(SLC_TOPK)) & (j < n_slc)).astype(F32)


def _overlap_matrix(n_rows, n_cols, n_cmp, n_slc):
    i = jnp.arange(n_rows)[:, None] * CMP_STRIDE
    j = jnp.arange(n_cols)[None, :] * SLC_LEN
    ok = (i < j + SLC_LEN) & (i + CMP_LEN > j) & (jnp.arange(n_rows)[:, None] < n_cmp) & (jnp.arange(n_cols)[None, :] < n_slc)
    return ok.astype(BF16)


def _cmp_attn_prompt(h, kvb, *, tq):
    M = h.shape[0]
    nb = M // SEQ
    nq = SEQ // tq
    n_cmp = SEQ // CMP_STRIDE - 1
    n_slc = SEQ // SLC_LEN
    ov = _overlap_matrix(LANES, LANES, n_cmp, n_slc)
    return pl.pallas_call(
        functools.partial(_cmp_attn_body, tq=tq, n_cmp=n_cmp, n_slc=n_slc),
        out_shape=(jax.ShapeDtypeStruct((M, NSA_Q), F32),
                   jax.ShapeDtypeStruct((nb, NSA_KV, SEQ, LANES), F32)),
        grid=(nb, NSA_KV, nq),
        in_specs=[pl.BlockSpec((tq, NSA_GROUP * NSA_HD), lambda b, g, i: (b * nq + i, g)),
                  pl.BlockSpec((1, 1, 1, LANES, NSA_HD), lambda b, g, i: (0, b, g, 0, 0)),
                  pl.BlockSpec((1, 1, 1, LANES, NSA_HD), lambda b, g, i: (1, b, g, 0, 0)),
                  pl.BlockSpec((LANES, LANES), lambda b, g, i: (0, 0))],
        out_specs=(pl.BlockSpec((tq, NSA_GROUP * NSA_HD), lambda b, g, i: (b * nq + i, g)),
                   pl.BlockSpec((1, 1, tq, LANES), lambda b, g, i: (b, g, i, 0))),
        compiler_params=_cparams(("parallel", "parallel", "arbitrary")),
        name="cmp_attn",
    )(h, kvb, kvb, ov)


def _dec_blocks_body(pt_ref, pool_ref, wc_ref, pe_ref, b1_ref, w2_ref, new_ref, o_ref, buf, sem, carry, *, layer):
    G = NSA_KV
    P = PAGES_PER_STEP
    mrows = 8 * P
    slab = mrows * CHUNK_PITCH
    b = pl.program_id(0)
    j = pl.program_id(1)
    nj = pl.num_programs(1)
    slot = j % 2

    def page_copy(jj, sl, p):
        phys = pt_ref[b, jj * P + p]
        return pltpu.make_async_copy(pool_ref.at[layer, phys],
                                     buf.at[sl, pl.ds(8 * p, 8), pl.ds(0, 64), :], sem.at[sl])

    @pl.when(j == 0)
    def _():
        for p in range(P):
            page_copy(0, 0, p).start()

    @pl.when(j + 1 < nj)
    def _():
        for p in range(P):
            page_copy(j + 1, 1 - slot, p).start()

    for p in range(P):
        page_copy(j, slot, p).wait()

    flat = buf.reshape(2 * slab, LANES)
    base_row = slot * slab
    wc = wc_ref[...]
    w2 = w2_ref[...]
    ks = 4
    pr = None
    for s0 in range(0, CMP_STRIDE, ks):
        xs = []
        for g in range(G):
            pieces = [flat[pl.ds(base_row + 4 * s + g, mrows, stride=CHUNK_PITCH), :].astype(BF16)
                      for s in range(s0, s0 + ks)]
            xs.append(jnp.concatenate(pieces, axis=1))
        xk = jnp.concatenate(xs, axis=0)
        d = jnp.dot(xk, wc[s0 * NSA_HD:(s0 + ks) * NSA_HD, :], preferred_element_type=F32)
        pr = d if pr is None else pr + d
    base = _cmp_base(pe_ref, wc, b1_ref)
    p0 = pr[:, :CMP_HID]
    p1 = pr[:, CMP_HID:]
    p1s = pltpu.roll(p1, G * mrows - 1, 0)
    kb = jnp.dot(jax.nn.gelu(base + p0 + p1s).astype(BF16), w2, preferred_element_type=F32)
    row0 = pl.multiple_of(j * mrows, mrows)
    for g in range(G):
        o_ref[0, g, pl.ds(row0, mrows), :] = kb[g * mrows:(g + 1) * mrows]

    zero4 = jnp.zeros((8 - G, CMP_HID), F32)

    def fix_rows(a, c):
        hfix = base + a + c
        return jnp.dot(jax.nn.gelu(hfix).astype(BF16), w2, preferred_element_type=F32)

    @pl.when(j > 0)
    def _():
        first = jnp.concatenate([p1[g * mrows:g * mrows + 1] for g in range(G)] + [zero4], axis=0)
        kf = fix_rows(carry[...], first)
        for g in range(G):
            o_ref[0, g, pl.ds(row0 - 1, 1), :] = kf[g:g + 1]

    lastp0 = jnp.concatenate([p0[(g + 1) * mrows - 1:(g + 1) * mrows] for g in range(G)] + [zero4], axis=0)
    carry[...] = lastp0

    @pl.when(j == nj - 1)
    def _():
        pn = jnp.dot(new_ref[0].astype(BF16), wc[0:NSA_HD, CMP_HID:], preferred_element_type=F32)
        kf = fix_rows(lastp0, pn)
        for g in range(G):
            o_ref[0, g, pl.ds(row0 + mrows - 1, 1), :] = kf[g:g + 1]


def _dec_blocks(page_table, pool5, layer, wcat, pe2, b1, w2, new_rows):
    nb = page_table.shape[0]
    P = PAGES_PER_STEP
    nj = N_PAGES // P
    grid_spec = pltpu.PrefetchScalarGridSpec(
        num_scalar_prefetch=1,
        grid=(nb, nj),
        in_specs=[pl.BlockSpec(memory_space=pl.ANY),
                  pl.BlockSpec((CMP_STRIDE * NSA_HD, 2 * CMP_HID), lambda b, j, pt: (0, 0)),
                  pl.BlockSpec((2, 8, CMP_STRIDE * NSA_HD), lambda b, j, pt: (0, 0, 0)),
                  pl.BlockSpec((1, CMP_HID), lambda b, j, pt: (0, 0)),
                  pl.BlockSpec((CMP_HID, NSA_HD), lambda b, j, pt: (0, 0)),
                  pl.BlockSpec((1, 8, NSA_HD), lambda b, j, pt: (b, 0, 0))],
        out_specs=pl.BlockSpec((1, NSA_KV, N_PAST_CHUNK, NSA_HD), lambda b, j, pt: (b, 0, 0, 0)),
        scratch_shapes=[pltpu.VMEM((2, 8 * P, CHUNK_PITCH, LANES), F32),
                        pltpu.SemaphoreType.DMA((2,)),
                        pltpu.VMEM((8, CMP_HID), F32)],
    )
    return pl.pallas_call(
        functools.partial(_dec_blocks_body, layer=layer),
        out_shape=jax.ShapeDtypeStruct((nb, NSA_KV, N_PAST_CHUNK, NSA_HD), F32),
        grid_spec=grid_spec,
        compiler_params=_cparams(("arbitrary", "arbitrary")),
        name="dec_blocks",
    )(page_table, pool5, wcat, pe2, b1, w2, new_rows)


def _dec_cmp_attn_body(q_ref, kb_ref, vb_ref, o_ref, idx_ref):
    G, R, hd = NSA_KV, NSA_GROUP, NSA_HD
    n_blk = N_PAST_CHUNK
    nn = lax.broadcasted_iota(jnp.int32, (n_blk, SLC_LANES), 0) * CMP_STRIDE
    jj = lax.broadcasted_iota(jnp.int32, (n_blk, SLC_LANES), 1)
    ov = ((nn < jj * SLC_LEN + SLC_LEN) & (nn + CMP_LEN > jj * SLC_LEN) & (jj < N_SLC_DEC)).astype(BF16)
    n = lax.broadcasted_iota(jnp.int32, (8, n_blk), 1)
    mask = (n * CMP_STRIDE + CMP_LEN - 1) <= PAST_LEN
    rowi = lax.broadcasted_iota(jnp.int32, (8, n_blk), 0)
    imps = []
    for g in range(G):
        q = q_ref[0, g].astype(BF16)
        s = lax.dot_general(q, kb_ref[0, g].astype(BF16), (((1,), (1,)), ((), ())),
                            preferred_element_type=F32) * (hd ** -0.5)
        s = jnp.where(mask, s, NEG)
        m = jnp.max(s, axis=-1, keepdims=True)
        e = jnp.where(mask, jnp.exp(s - m), 0.0)
        l = jnp.sum(e, axis=-1, keepdims=True)
        p = e / jnp.where(l > 0.0, l, 1.0)
        o_ref[0, g] = jnp.dot(p.astype(BF16), vb_ref[0, g].astype(BF16), preferred_element_type=F32)
        psum = jnp.sum(jnp.where(rowi < R, p, 0.0), axis=0, keepdims=True)
        imp8 = jnp.dot(jnp.broadcast_to(psum, (8, n_blk)).astype(BF16), ov, preferred_element_type=F32)
        imps.append(imp8[0:1])
    imp = jnp.concatenate(imps + [jnp.zeros((8 - G, SLC_LANES), F32)], axis=0)
    j = lax.broadcasted_iota(jnp.int32, (8, SLC_LANES), 1)
    cur = PAST_LEN // SLC_LEN
    forced = (j == 0) | (j == cur) | (j == cur - 1)
    imp = jnp.where(forced, FORCE_SCORE, jnp.where(j <= cur, imp, -1.0))
    imp = jnp.where(j < N_SLC_DEC, imp, -jnp.inf)
    lane = lax.broadcasted_iota(jnp.int32, (8, LANES), 1)
    jf = j.astype(F32)
    out = jnp.zeros((8, LANES), F32)
    for k in range(SLC_TOPK):
        mx = jnp.max(imp, axis=-1, keepdims=True)
        pick = jnp.min(jnp.where(imp == mx, jf, float(SLC_LANES)), axis=-1, keepdims=True)
        out = jnp.where(lane == k, pick, out)
        imp = jnp.where(jf == pick, -jnp.inf, imp)
    idx_ref[0] = out.astype(jnp.int32)


def _dec_cmp_attn(q, kb, vb):
    nb = q.shape[0]
    blk = pl.BlockSpec((1, NSA_KV, N_PAST_CHUNK, NSA_HD), lambda b: (b, 0, 0, 0))
    qspec = pl.BlockSpec((1, NSA_KV, 8, NSA_HD), lambda b: (b, 0, 0, 0))
    return pl.pallas_call(
        _dec_cmp_attn_body,
        out_shape=(jax.ShapeDtypeStruct((nb, NSA_KV, 8, NSA_HD), F32),
                   jax.ShapeDtypeStruct((nb, 8, LANES), jnp.int32)),
        grid=(nb,),
        in_specs=[qspec, blk, blk],
        out_specs=(qspec, pl.BlockSpec((1, 8, LANES), lambda b: (b, 0, 0))),
        compiler_params=_cparams(("parallel",)),
        name="dec_cmp_attn",
    )(q, kb, vb)


def _dec_slc_body(idx_ref, pt_ref, q_ref, kn_ref, vn_ref, pk_ref, pv_ref, o_ref, kbuf, vbuf, sem, *, layer):
    G, R, hd = NSA_KV, NSA_GROUP, NSA_HD
    K = SLC_TOPK
    b = pl.program_id(0)

    def copies(g, k):
        jsel = idx_ref[(b * G + g) * K + k]
        jp = jnp.minimum(jsel, N_PAST_BLK - 1)
        phys = pt_ref[b, lax.shift_right_logical(jp, 1)]
        rows = pl.ds((jp & 1) * SLC_LEN, SLC_LEN)
        dst = pl.ds((g * K + k) * SLC_LEN, SLC_LEN)
        ck = pltpu.make_async_copy(pk_ref.at[layer, phys, rows, g, :], kbuf.at[dst, :], sem.at[0])
        cv = pltpu.make_async_copy(pv_ref.at[layer, phys, rows, g, :], vbuf.at[dst, :], sem.at[1])
        return ck, cv

    for g in range(G):
        for k in range(K):
            ck, cv = copies(g, k)
            ck.start()
            cv.start()
    for g in range(G):
        for k in range(K):
            ck, cv = copies(g, k)
            ck.wait()
            cv.wait()

    nkeys = K * SLC_LEN
    slot = lax.broadcasted_iota(jnp.int32, (8, nkeys), 1) // SLC_LEN
    scale = hd ** -0.5
    for g in range(G):
        valid = jnp.zeros((8, nkeys), jnp.int32)
        n_new_i = jnp.int32(0)
        for k in range(K):
            jsel = idx_ref[(b * G + g) * K + k]
            past_i = jnp.where(jsel < N_PAST_BLK, 1, 0).astype(jnp.int32)
            valid = jnp.where(slot == k, past_i, valid)
            n_new_i = n_new_i + (1 - past_i)
        mask = valid > 0
        n_new = jnp.full((8, 1), n_new_i, jnp.int32).astype(F32)
        q = q_ref[0, g].astype(BF16)
        kg = kbuf[g * nkeys:(g + 1) * nkeys, :].astype(BF16)
        vg = vbuf[g * nkeys:(g + 1) * nkeys, :].astype(BF16)
        s = lax.dot_general(q, kg, (((1,), (1,)), ((), ())), preferred_element_type=F32) * scale
        s = jnp.where(mask, s, NEG)
        kn = kn_ref[0, g][0:1].astype(BF16).astype(F32)
        vn = vn_ref[0, g][0:1].astype(BF16).astype(F32)
        has_new = n_new > 0.5
        s_new = jnp.sum(q.astype(F32) * kn, axis=-1, keepdims=True) * scale
        s_new = jnp.where(has_new, s_new, NEG)
        m = jnp.maximum(jnp.max(s, axis=-1, keepdims=True), s_new)
        e = jnp.where(mask, jnp.exp(s - m), 0.0)
        e_new = jnp.where(has_new, jnp.exp(s_new - m), 0.0) * n_new
        l = jnp.sum(e, axis=-1, keepdims=True) + e_new
        pv = jnp.dot(e.astype(BF16), vg, preferred_element_type=F32) + e_new * vn
        o_ref[0, g] = pv / l


def _dec_slc(idx_flat, page_table, q, kn, vn, pool_k, pool_v, layer):
    nb = q.shape[0]
    qspec = pl.BlockSpec((1, NSA_KV, 8, NSA_HD), lambda b, idx, pt: (b, 0, 0, 0))
    nrow = NSA_KV * SLC_TOPK * SLC_LEN
    grid_spec = pltpu.PrefetchScalarGridSpec(
        num_scalar_prefetch=2,
        grid=(nb,),
        in_specs=[qspec, qspec, qspec, pl.BlockSpec(memory_space=pl.ANY), pl.BlockSpec(memory_space=pl.ANY)],
        out_specs=qspec,
        scratch_shapes=[pltpu.VMEM((nrow, NSA_HD), F32), pltpu.VMEM((nrow, NSA_HD), F32),
                        pltpu.SemaphoreType.DMA((2,))],
    )
    return pl.pallas_call(
        functools.partial(_dec_slc_body, layer=layer),
        out_shape=jax.ShapeDtypeStruct((nb, NSA_KV, 8, NSA_HD), F32),
        grid_spec=grid_spec,
        compiler_params=_cparams(("arbitrary",)),
        name="dec_slc",
    )(idx_flat, page_table, q, kn, vn, pool_k, pool_v)


def _dec_win_body(*refs, G, R, hd, L, window, use_sink):
    if use_sink:
        q_ref, kb_ref, vb_ref, kn_ref, vn_ref, sink_ref, o_ref = refs
    else:
        q_ref, kb_ref, vb_ref, kn_ref, vn_ref, o_ref = refs
    scale = hd ** -0.5
    i = lax.broadcasted_iota(jnp.int32, (8, L), 1)
    mask = (L - i) < window
    for g in range(G):
        q = q_ref[0, g].astype(BF16)
        s = lax.dot_general(q, kb_ref[0, g].astype(BF16), (((1,), (1,)), ((), ())),
                            preferred_element_type=F32) * scale
        s = jnp.where(mask, s, NEG)
        kn = kn_ref[0, g][0:1].astype(BF16).astype(F32)
        vn = vn_ref[0, g][0:1].astype(BF16).astype(F32)
        s_new = jnp.sum(q.astype(F32) * kn, axis=-1, keepdims=True) * scale
        m = jnp.maximum(jnp.max(s, axis=-1, keepdims=True), s_new)
        if use_sink:
            sk = sink_ref[g][:, 0:1]
            m = jnp.maximum(m, sk)
        e = jnp.where(mask, jnp.exp(s - m), 0.0)
        e_new = jnp.exp(s_new - m)
        l = jnp.sum(e, axis=-1, keepdims=True) + e_new
        if use_sink:
            l = l + jnp.exp(sk - m)
        pv = jnp.dot(e.astype(BF16), vb_ref[0, g].astype(BF16), preferred_element_type=F32) + e_new * vn
        o_ref[0, g] = pv / l


def _dec_win(q, kb, vb, kn, vn, sinks, *, window):
    nb, G, _, hd = q.shape
    L = kb.shape[2]
    R = 8
    qspec = pl.BlockSpec((1, G, 8, hd), lambda b: (b, 0, 0, 0))
    bspec = pl.BlockSpec((1, G, L, hd), lambda b: (b, 0, 0, 0))
    in_specs = [qspec, bspec, bspec, qspec, qspec]
    args = [q, kb, vb, kn, vn]
    if sinks is not None:
        in_specs.append(pl.BlockSpec((G, 8, LANES), lambda b: (0, 0, 0)))
        args.append(sinks)
    return pl.pallas_call(
        functools.partial(_dec_win_body, G=G, R=R, hd=hd, L=L, window=window, use_sink=sinks is not None),
        out_shape=jax.ShapeDtypeStruct((nb, G, 8, hd), F32),
        grid=(nb,),
        in_specs=in_specs,
        out_specs=qspec,
        compiler_params=_cparams(("parallel",)),
        name="dec_win",
    )(*args)


def _cmp_params(pe, w1, b1, w2):
    wcat = jnp.transpose(w1.reshape(2, CMP_STRIDE * NSA_HD, CMP_HID), (1, 0, 2)).reshape(CMP_STRIDE * NSA_HD, 2 * CMP_HID)
    pe2 = jnp.broadcast_to(pe.reshape(2, 1, CMP_STRIDE * NSA_HD), (2, 8, CMP_STRIDE * NSA_HD))
    return wcat.astype(BF16), pe2.astype(BF16), b1.reshape(1, CMP_HID), w2.astype(BF16)


def _group_rows(x, G, R, hd):
    nb = x.shape[0]
    x = x.reshape(nb, G, R, hd)
    if R < 8:
        x = jnp.pad(x, ((0, 0), (0, 0), (0, 8 - R), (0, 0)))
    return x


def _nsa_layer(a, xp, xb, xs, xsb, caches, page_table, win_k, win_v, w_main, w_gate, w_o, cmpk, cmpv, ln_g, ln_b,
               tabs_p, tabs_s):
    pool_ck, pool_cv, pool_sk, pool_sv = caches
    nb = DEC_BATCH
    half = NSA_ROT // 2
    h = _mm(xb, w_main, tm=1024, tn=512)
    hg = _mm(xb, w_gate, tm=1024, tn=LANES)
    wcat = jnp.stack([cmpk[0], cmpv[0]])
    pe2 = jnp.stack([cmpk[1], cmpv[1]])
    b1 = jnp.stack([cmpk[2], cmpv[2]])
    w2 = jnp.stack([cmpk[3], cmpv[3]])
    kvb = _cmp_mlp_prompt(h, wcat, pe2, b1, w2)
    o_cmp, sel = _cmp_attn_prompt(h, kvb, tq=256)
    q_rot = _rope(h, 0, NSA_Q, tabs_p, half=half, tr=512, rows_per_seq=SEQ)
    ks_rot = _rope(h, NSA_Q + 2 * NSA_KVD, NSA_KVD, tabs_p, half=half, tr=512, rows_per_seq=SEQ)
    kw_rot = _rope(h, NSA_Q + 4 * NSA_KVD, NSA_KVD, tabs_p, half=half, tr=512, rows_per_seq=SEQ)
    vs_cb = (NSA_Q + 3 * NSA_KVD) // NSA_HD
    vw_cb = (NSA_Q + 5 * NSA_KVD) // NSA_HD
    o_slc = _attn_slc(q_rot, ks_rot, h, vs_cb, sel, tq=256)
    o_win = _attn_win(q_rot, kw_rot, h, vw_cb, tq=256)
    yp, ypb = _out_ln([o_cmp, o_slc, o_win], hg, w_o, xp, ln_g, ln_b, tm=256, tk=NSA_Q)

    def kv5(arr):
        return arr.reshape(BATCH, SEQ, NSA_KV, NSA_HD)

    def hs(i):
        return h[:, NSA_Q + i * NSA_KVD:NSA_Q + (i + 1) * NSA_KVD]

    lw = min(NSA_WINDOW, SEQ)
    st_p = (kv5(hs(0)), kv5(hs(1)), kv5(ks_rot), kv5(hs(3)), kv5(kw_rot)[:, SEQ - lw:], kv5(hs(5))[:, SEQ - lw:])

    hd_ = _mm(xsb, w_main, tm=nb, tn=512)
    hgd = _mm(xsb, w_gate, tm=nb, tn=LANES)
    qd_rot = _rope(hd_, 0, NSA_Q, tabs_s, half=half, tr=nb, rows_per_seq=nb)
    ksd_rot = _rope(hd_, NSA_Q + 2 * NSA_KVD, NSA_KVD, tabs_s, half=half, tr=nb, rows_per_seq=nb)
    kwd_rot = _rope(hd_, NSA_Q + 4 * NSA_KVD, NSA_KVD, tabs_s, half=half, tr=nb, rows_per_seq=nb)

    def hsd(i):
        return hd_[:, NSA_Q + i * NSA_KVD:NSA_Q + (i + 1) * NSA_KVD]

    kc_new, vc_new, vs_new, vw_new = hsd(0), hsd(1), hsd(3), hsd(5)
    def rows8(x):
        return jnp.pad(x.reshape(nb, NSA_KV, NSA_HD), ((0, 0), (0, 8 - NSA_KV), (0, 0)))

    kblk = _dec_blocks(page_table, pool_ck, a, cmpk[0], cmpk[1], cmpk[2], cmpk[3], rows8(kc_new))
    vblk = _dec_blocks(page_table, pool_cv, a, cmpv[0], cmpv[1], cmpv[2], cmpv[3], rows8(vc_new))
    qd = _group_rows(hd_[:, :NSA_Q], NSA_KV, NSA_GROUP, NSA_HD)
    qd_r = _group_rows(qd_rot, NSA_KV, NSA_GROUP, NSA_HD)
    od_cmp, idx = _dec_cmp_attn(qd, kblk, vblk)
    idx_flat = idx[:, :NSA_KV, :SLC_TOPK].reshape(-1)
    od_slc = _dec_slc(idx_flat, page_table, qd_r, _group_rows(ksd_rot, NSA_KV, 1, NSA_HD),
                      _group_rows(vs_new, NSA_KV, 1, NSA_HD), pool_sk, pool_sv, a)
    wk_hm = jnp.transpose(win_k, (0, 2, 1, 3))
    wv_hm = jnp.transpose(win_v, (0, 2, 1, 3))
    od_win = _dec_win(qd_r, wk_hm, wv_hm, _group_rows(kwd_rot, NSA_KV, 1, NSA_HD),
                      _group_rows(vw_new, NSA_KV, 1, NSA_HD), None, window=NSA_WINDOW)

    def flat_heads(o):
        return o[:, :, :NSA_GROUP, :].reshape(nb, NSA_Q)

    ys, ysb = _out_ln([flat_heads(od_cmp), flat_heads(od_slc), flat_heads(od_win)], hgd, w_o, xs, ln_g, ln_b,
                      tm=nb, tk=NSA_Q)

    def new5(arr):
        return arr.reshape(nb, 1, NSA_KV, NSA_HD)

    st_s = (new5(kc_new), new5(vc_new), new5(ksd_rot), new5(vs_new),
            jnp.concatenate([win_k[:, 1:], new5(kwd_rot)], axis=1),
            jnp.concatenate([win_v[:, 1:], new5(vw_new)], axis=1))
    return yp, ypb, ys, ysb, st_p, st_s


def _swa_layer(xp, xb, xs, xsb, buf_k, buf_v, w_in, b_in, w_pad, b_pad, sinks, w_o, ln_g, ln_b, tabs_p, tabs_pk, tabs_s):
    nb = DEC_BATCH
    half = SWA_ROT // 2
    G, R, hd = SWA_KV, SWA_GROUP, SWA_HD
    kpad = G * LANES
    h = _mm(xb, w_pad, b_pad, tm=1024, tn=512)
    q_rot = _rope(h, 0, SWA_Q, tabs_p, half=half, tr=512, rows_per_seq=SEQ)
    k_rot = _rope(h, SWA_Q, kpad, tabs_pk, half=half, tr=512, rows_per_seq=SEQ)
    sink_p = jnp.transpose(sinks.reshape(G, R // 2, 2), (0, 2, 1)).reshape(G, R, 1, 1)
    sink_p = jnp.broadcast_to(sink_p, (G, R, 8, LANES))
    o = _attn_swa(q_rot, k_rot, h, (SWA_Q + kpad) // LANES, sink_p, tq=128)
    yp, ypb = _out_ln([o], None, w_o, xp, ln_g, ln_b, tm=512, tk=SWA_Q)
    lw = min(SWA_WINDOW, SEQ)
    v_pad = h[:, SWA_Q + kpad:]
    st_p = (k_rot.reshape(BATCH, SEQ, G, LANES)[:, SEQ - lw:, :, :hd],
            v_pad.reshape(BATCH, SEQ, G, LANES)[:, SEQ - lw:, :, :hd])

    hd_ = _mm(xsb, w_in, b_in, tm=nb, tn=512)
    qd_rot = _rope(hd_, 0, SWA_Q, tabs_s, half=half, tr=nb, rows_per_seq=nb)
    kd_rot = _rope(hd_, SWA_Q, SWA_KVD, tabs_s, half=half, tr=nb, rows_per_seq=nb)
    vd = hd_[:, SWA_Q + SWA_KVD:]
    sink_s = jnp.broadcast_to(sinks.reshape(G, R, 1), (G, R, LANES))
    od = _dec_win(_group_rows(qd_rot, G, R, hd), jnp.transpose(buf_k, (0, 2, 1, 3)), jnp.transpose(buf_v, (0, 2, 1, 3)),
                  _group_rows(kd_rot, G, 1, hd), _group_rows(vd, G, 1, hd), sink_s, window=SWA_WINDOW)
    ys, ysb = _out_ln([od.reshape(nb, SWA_Q)], None, w_o, xs, ln_g, ln_b, tm=nb, tk=SWA_Q)
    kn5 = kd_rot.reshape(nb, 1, G, hd)
    vn5 = vd.reshape(nb, 1, G, hd)
    st_s = (jnp.concatenate([buf_k[:, 1:], kn5], axis=1), jnp.concatenate([buf_v[:, 1:], vn5], axis=1))
    return yp, ypb, ys, ysb, st_p, st_s


def _ffn_layer(xp, xb, xs, xsb, state, w_up, conv_w, conv_b, w_down, ln_g, ln_b):
    nb = DEC_BATCH
    yp, ypb, sa, sg = _ffn_prompt(xb, xp, w_up, conv_w, conv_b, w_down, ln_g, ln_b)
    tps = sa.shape[0] // BATCH
    conv_p = jnp.concatenate([sa[tps - 1::tps, 6:8], sg[tps - 1::tps, 6:8]], axis=-1)
    hup = _mm(xsb, w_up, tm=nb, tn=512)
    gated = _dec_gate(hup, state[:, 0], state[:, 1], conv_w, conv_b)
    ys, ysb = _out_ln([gated], None, w_down, xs, ln_g, ln_b, tm=nb, tk=512)
    conv_s = jnp.concatenate([state[:, 1:], hup[:, None, :]], axis=1)
    return yp, ypb, ys, ysb, conv_p, conv_s


def _swa_padded(w_in, b_in):
    def widen(t):
        lead = t.shape[:-1]
        t = t.reshape(lead + (SWA_KV, SWA_HD))
        t = jnp.pad(t, [(0, 0)] * len(lead) + [(0, 0), (0, LANES - SWA_HD)])
        return t.reshape(lead + (SWA_KV * LANES,))

    def cols(t):
        return jnp.concatenate([t[..., :SWA_Q], widen(t[..., SWA_Q:SWA_Q + SWA_KVD]), widen(t[..., SWA_Q + SWA_KVD:])], axis=-1)

    return cols(w_in), cols(b_in)


def kernel(x_prompt, x_sample, cache_nsa_cmp_k, cache_nsa_cmp_v, cache_nsa_slc_k, cache_nsa_slc_v, state_nsa_win_k, state_nsa_win_v, state_swa_k, state_swa_v, state_conv, page_table, nsa_w_in, nsa_w_o, nsa_cmp_pe_k, nsa_cmp_w1_k, nsa_cmp_b1_k, nsa_cmp_w2_k, nsa_cmp_pe_v, nsa_cmp_w1_v, nsa_cmp_b1_v, nsa_cmp_w2_v, swa_w_in, swa_b_in, swa_sinks, swa_w_o, ln1_g, ln1_b, ln2_g, ln2_b, ffn_w_up, ffn_conv_w, ffn_conv_b, ffn_w_down):
    n_nsa, n_pool = cache_nsa_cmp_k.shape[:2]
    pool_ck = cache_nsa_cmp_k.reshape(n_nsa, n_pool, 8, 64, NSA_HD)
    pool_cv = cache_nsa_cmp_v.reshape(n_nsa, n_pool, 8, 64, NSA_HD)
    pool_sk = cache_nsa_slc_k
    pool_sv = cache_nsa_slc_v

    pos_p = jnp.arange(SEQ)
    pos_s = jnp.full((DEC_BATCH,), PAST_LEN)
    nsa_tabs_p = _rope_tables(pos_p, NSA_ROT, NSA_HD)
    nsa_tabs_s = _rope_tables(pos_s, NSA_ROT, NSA_HD)
    swa_tabs_p = _rope_tables(pos_p, SWA_ROT, SWA_HD)
    swa_tabs_s = _rope_tables(pos_s, SWA_ROT, SWA_HD)
    swa_tabs_pk = _rope_tables(pos_p, SWA_ROT, LANES)

    yp = x_prompt.reshape(BATCH * SEQ, D_MODEL)
    ys = x_sample.reshape(DEC_BATCH, D_MODEL)
    ypb, ysb = yp.astype(BF16), ys.astype(BF16)
    nsa_p, nsa_s, swa_p, swa_s, conv_p, conv_s = [], [], [], [], [], []
    for i in range(DEPTH):
        a = i // 2
        g1, b1 = ln1_g[i].reshape(1, D_MODEL), ln1_b[i].reshape(1, D_MODEL)
        g2, b2 = ln2_g[i].reshape(1, D_MODEL), ln2_b[i].reshape(1, D_MODEL)
        if i % 2 == 0:
            w_in = nsa_w_in[a]
            w_main = w_in[:, :NSA_MAIN].astype(BF16)
            w_gate = jnp.pad(w_in[:, NSA_MAIN:], ((0, 0), (0, LANES - 3 * NSA_HEADS))).astype(BF16)
            cmpk = _cmp_params(nsa_cmp_pe_k[a], nsa_cmp_w1_k[a], nsa_cmp_b1_k[a], nsa_cmp_w2_k[a])
            cmpv = _cmp_params(nsa_cmp_pe_v[a], nsa_cmp_w1_v[a], nsa_cmp_b1_v[a], nsa_cmp_w2_v[a])
            yp, ypb, ys, ysb, st_p, st_s = _nsa_layer(
                a, yp, ypb, ys, ysb, (pool_ck, pool_cv, pool_sk, pool_sv), page_table,
                state_nsa_win_k[a], state_nsa_win_v[a],
                w_main, w_gate, nsa_w_o[a].astype(BF16), cmpk, cmpv, g1, b1, nsa_tabs_p, nsa_tabs_s)
            nsa_p.append(st_p)
            nsa_s.append(st_s)
        else:
            w_pad, b_pad = _swa_padded(swa_w_in[a], swa_b_in[a].reshape(1, -1))
            yp, ypb, ys, ysb, st_p, st_s = _swa_layer(
                yp, ypb, ys, ysb, state_swa_k[a], state_swa_v[a], swa_w_in[a].astype(BF16), swa_b_in[a].reshape(1, -1),
                w_pad.astype(BF16), b_pad, swa_sinks[a], swa_w_o[a].astype(BF16), g1, b1,
                swa_tabs_p, swa_tabs_pk, swa_tabs_s)
            swa_p.append(st_p)
            swa_s.append(st_s)
        yp, ypb, ys, ysb, cp, cs = _ffn_layer(yp, ypb, ys, ysb, state_conv[i], ffn_w_up[i].astype(BF16), ffn_conv_w[i],
                                              ffn_conv_b[i].reshape(1, -1), ffn_w_down[i].astype(BF16), g2, b2)
        conv_p.append(cp)
        conv_s.append(cs)

    p_cmp_k, p_cmp_v, p_slc_k, p_slc_v, p_nwin_k, p_nwin_v = [jnp.stack(t) for t in zip(*nsa_p)]
    s_cmp_k, s_cmp_v, s_slc_k, s_slc_v, s_nwin_k, s_nwin_v = [jnp.stack(t) for t in zip(*nsa_s)]
    p_swa_k, p_swa_v = [jnp.stack(t) for t in zip(*swa_p)]
    s_swa_k, s_swa_v = [jnp.stack(t) for t in zip(*swa_s)]
    return (yp.reshape(BATCH, SEQ, D_MODEL), ys.reshape(DEC_BATCH, 1, D_MODEL),
            p_cmp_k, p_cmp_v, p_slc_k, p_slc_v, p_nwin_k, p_nwin_v, p_swa_k, p_swa_v, jnp.stack(conv_p),
            s_cmp_k, s_cmp_v, s_slc_k, s_slc_v, s_nwin_k, s_nwin_v, s_swa_k, s_swa_v, jnp.stack(conv_s))
```

```python
import functools
import math

import jax
import jax.numpy as jnp
from jax import lax
from jax.experimental import pallas as pl
from jax.experimental.pallas import tpu as pltpu

F32 = jnp.float32
BF16 = jnp.bfloat16

D_MODEL = 2048
BATCH = 4
SEQ = 2048
DEPTH = 4
DEC_BATCH = 32
PAST_LEN = 16384
PAGE_SIZE = 128

NSA_HEADS = 16
NSA_KV = 4
NSA_HD = 128
NSA_GROUP = 4
NSA_ROT = 32
NSA_Q = 2048
NSA_KVD = 512
NSA_MAIN = NSA_Q + 6 * NSA_KVD
CMP_LEN = 32
CMP_STRIDE = 16
CMP_HID = 256
SLC_LEN = 64
SLC_TOPK = 16
NSA_WINDOW = 512
FORCE_SCORE = 1e4
SWA_HEADS = 32
SWA_KV = 4
SWA_HD = 64
SWA_GROUP = 8
SWA_ROT = 16
SWA_Q = 2048
SWA_KVD = 256
SWA_WINDOW = 128
ROPE_THETA = 500000.0
D_FF = 5632
ALPHA = (2 * DEPTH) ** 0.25
LN_EPS = 1e-5
NEG = -1e30

LANES = 128
PAGES_PER_STEP = 32
CHUNK_PITCH = 72
N_PAGES = PAST_LEN // PAGE_SIZE
N_PAST_CHUNK = PAST_LEN // CMP_STRIDE
N_PAST_BLK = PAST_LEN // SLC_LEN
N_SLC_DEC = N_PAST_BLK + 1
SLC_LANES = 384
VMEM_LIMIT = 56 * 1024 * 1024


def _cparams(sem):
    return pltpu.CompilerParams(dimension_semantics=sem, vmem_limit_bytes=VMEM_LIMIT)


def _ln(z, g, b):
    mu = jnp.mean(z, axis=-1, keepdims=True)
    d = z - mu
    var = jnp.mean(d * d, axis=-1, keepdims=True)
    return d * lax.rsqrt(var + LN_EPS) * g + b


def _mm_body(*refs, has_bias):
    if has_bias:
        x_ref, w_ref, b_ref, o_ref = refs
    else:
        x_ref, w_ref, o_ref = refs
    acc = jnp.dot(x_ref[...], w_ref[...], preferred_element_type=F32)
    if has_bias:
        acc = acc + b_ref[...]
    o_ref[...] = acc


def _w_spec(w, layer, rows, cols, index):
    if w.ndim == 2:
        return pl.BlockSpec((rows, cols), index)
    return pl.BlockSpec((None, rows, cols), lambda *a: (layer,) + tuple(index(*a)))


def _mm(x, w, bias=None, *, tm, tn, layer=None):
    M, K = x.shape
    N = w.shape[-1]
    in_specs = [pl.BlockSpec((tm, K), lambda i, j: (i, 0)),
                _w_spec(w, layer, K, tn, lambda i, j: (0, j))]
    args = [x, w]
    if bias is not None:
        in_specs.append(pl.BlockSpec((1, tn), lambda i, j: (0, j)))
        args.append(bias)
    return pl.pallas_call(
        functools.partial(_mm_body, has_bias=bias is not None),
        out_shape=jax.ShapeDtypeStruct((M, N), F32),
        grid=(M // tm, N // tn),
        in_specs=in_specs,
        out_specs=pl.BlockSpec((tm, tn), lambda i, j: (i, j)),
        compiler_params=_cparams(("parallel", "arbitrary")),
        name="mm",
    )(*args)


def _out_ln_body(*refs, n_lhs, gated, nk):
    lhs_refs = refs[:n_lhs]
    pos = n_lhs
    gate_ref = None
    if gated:
        gate_ref = refs[pos]
        pos += 1
    w_ref, res_ref, g_ref, b_ref, y_ref, yb_ref, acc_ref = refs[pos:pos + 7]
    k = pl.program_id(1)
    if gated:
        sig = jax.nn.sigmoid(gate_ref[...])
        vals = [r[...] for r in lhs_refs]
        cols = []
        for h in range(NSA_HEADS):
            sl = slice(h * NSA_HD, (h + 1) * NSA_HD)
            m = None
            for br in range(3):
                c = br * NSA_HEADS + h
                t = sig[:, c:c + 1] * vals[br][:, sl]
                m = t if m is None else m + t
            cols.append(m)
        lhs = jnp.concatenate(cols, axis=1).astype(BF16)
    else:
        lhs = lhs_refs[0][...].astype(BF16)
    contrib = jnp.dot(lhs, w_ref[...], preferred_element_type=F32)

    @pl.when(k == 0)
    def _():
        acc_ref[...] = contrib

    @pl.when(k > 0)
    def _():
        acc_ref[...] += contrib

    @pl.when(k == nk - 1)
    def _():
        y = _ln(ALPHA * res_ref[...] + acc_ref[...], g_ref[...], b_ref[...])
        y_ref[...] = y
        yb_ref[...] = y.astype(BF16)


def _out_ln(lhs_list, gate, w, res, g, b, *, tm, tk, layer=None):
    M, K = lhs_list[0].shape
    N = w.shape[-1]
    nk = K // tk
    gated = gate is not None
    assert not gated or nk == 1
    in_specs = [pl.BlockSpec((tm, tk), lambda i, k: (i, k)) for _ in lhs_list]
    args = list(lhs_list)
    if gated:
        in_specs.append(pl.BlockSpec((tm, LANES), lambda i, k: (i, 0)))
        args.append(gate)
    in_specs += [_w_spec(w, layer, tk, N, lambda i, k: (k, 0)),
                 pl.BlockSpec((tm, N), lambda i, k: (i, 0)),
                 pl.BlockSpec((1, N), lambda i, k: (0, 0)),
                 pl.BlockSpec((1, N), lambda i, k: (0, 0))]
    args += [w, res, g, b]
    return pl.pallas_call(
        functools.partial(_out_ln_body, n_lhs=len(lhs_list), gated=gated, nk=nk),
        out_shape=(jax.ShapeDtypeStruct((M, N), F32), jax.ShapeDtypeStruct((M, N), BF16)),
        grid=(M // tm, nk),
        in_specs=in_specs,
        out_specs=(pl.BlockSpec((tm, N), lambda i, k: (i, 0)), pl.BlockSpec((tm, N), lambda i, k: (i, 0))),
        scratch_shapes=[pltpu.VMEM((tm, N), F32)],
        compiler_params=_cparams(("parallel", "arbitrary")),
        name="out_ln",
    )(*args)


def _ffn_body(xb_ref, xr_ref, wa_ref, wg_ref, cwa_ref, cwg_ref, cba_ref, cbg_ref, wd_ref, g_ref, b_ref,
              y_ref, yb_ref, sa_ref, sg_ref, acc_ref, carry_a, carry_g, *, tiles_per_seq, nj, tm, tc, cw_split):
    i = pl.program_id(0)
    j = pl.program_id(1)
    first = (i % tiles_per_seq) == 0
    @pl.when(first)
    def _():
        carry_a[j] = jnp.zeros(carry_a.shape[1:], F32)
        carry_g[j] = jnp.zeros(carry_g.shape[1:], F32)

    row = lax.broadcasted_iota(jnp.int32, (8, cw_split), 0)

    def branch(cs, w_ref, cw_ref, cb_ref, carry, st_ref):
        h = jnp.dot(xb_ref[...], w_ref[:, cs], preferred_element_type=F32)
        prev = carry[j, :, cs]
        r1 = pltpu.roll(h, 1, 0)
        r2 = pltpu.roll(h, 2, 0)
        h1 = jnp.concatenate([jnp.where(row == 0, prev[7:8], r1[0:8]), r1[8:]], axis=0)
        h2 = jnp.concatenate([jnp.where(row == 0, prev[6:7], jnp.where(row == 1, prev[7:8], r2[0:8])),
                              r2[8:]], axis=0)
        cw = cw_ref[:, cs]
        c = cb_ref[:, cs] + h2 * cw[0:1] + h1 * cw[1:2] + h * cw[2:3]
        last = h[tm - 8:tm]
        carry[j, :, cs] = last
        st_ref[0, :, cs] = last
        return c

    gated = []
    for c0 in range(0, tc, cw_split):
        cs = slice(c0, c0 + cw_split)
        ca = branch(cs, wa_ref, cwa_ref, cba_ref, carry_a, sa_ref)
        cg = branch(cs, wg_ref, cwg_ref, cbg_ref, carry_g, sg_ref)
        gated.append((ca * jax.nn.gelu(cg)).astype(BF16))
    gated = jnp.concatenate(gated, axis=1)

    @pl.when(j == 0)
    def _():
        acc_ref[...] = jnp.zeros(acc_ref.shape, F32)

    for n0 in range(0, D_MODEL, 512):
        ns = slice(n0, n0 + 512)
        acc_ref[:, ns] += jnp.dot(gated, wd_ref[:, ns], preferred_element_type=F32)

    @pl.when(j == nj - 1)
    def _():
        y = _ln(ALPHA * xr_ref[...] + acc_ref[...], g_ref[...], b_ref[...])
        y_ref[...] = y
        yb_ref[...] = y.astype(BF16)


def _ffn_prompt(xb, xr, w_up, conv_w, conv_b, w_down, g, b, *, tm=512, tc=512, layer=None):
    M = xb.shape[0]
    nj = D_FF // tc
    nb = M // tm
    tiles_per_seq = SEQ // tm
    y, yb, sa, sg = pl.pallas_call(
        functools.partial(_ffn_body, tiles_per_seq=tiles_per_seq, nj=nj, tm=tm, tc=tc, cw_split=256),
        out_shape=(jax.ShapeDtypeStruct((M, D_MODEL), F32),
                   jax.ShapeDtypeStruct((M, D_MODEL), BF16),
                   jax.ShapeDtypeStruct((nb, 8, D_FF), F32),
                   jax.ShapeDtypeStruct((nb, 8, D_FF), F32)),
        grid=(M // tm, nj),
        in_specs=[pl.BlockSpec((tm, D_MODEL), lambda i, j: (i, 0)),
                  pl.BlockSpec((tm, D_MODEL), lambda i, j: (i, 0)),
                  _w_spec(w_up, layer, D_MODEL, tc, lambda i, j: (0, j)),
                  _w_spec(w_up, layer, D_MODEL, tc, lambda i, j: (0, j + nj)),
                  pl.BlockSpec((3, tc), lambda i, j: (0, j)),
                  pl.BlockSpec((3, tc), lambda i, j: (0, j + nj)),
                  pl.BlockSpec((1, tc), lambda i, j: (0, j)),
                  pl.BlockSpec((1, tc), lambda i, j: (0, j + nj)),
                  _w_spec(w_down, layer, tc, D_MODEL, lambda i, j: (j, 0)),
                  pl.BlockSpec((1, D_MODEL), lambda i, j: (0, 0)),
                  pl.BlockSpec((1, D_MODEL), lambda i, j: (0, 0))],
        out_specs=(pl.BlockSpec((tm, D_MODEL), lambda i, j: (i, 0)),
                   pl.BlockSpec((tm, D_MODEL), lambda i, j: (i, 0)),
                   pl.BlockSpec((1, 8, tc), lambda i, j: (i, 0, j)),
                   pl.BlockSpec((1, 8, tc), lambda i, j: (i, 0, j))),
        scratch_shapes=[pltpu.VMEM((tm, D_MODEL), F32),
                        pltpu.VMEM((nj, 8, tc), F32),
                        pltpu.VMEM((nj, 8, tc), F32)],
        compiler_params=_cparams(("arbitrary", "arbitrary")),
        name="ffn",
    )(xb, xr, w_up, w_up, conv_w, conv_w, conv_b, conv_b, w_down, g, b)
    return y, yb, sa, sg


def _dec_gate_body(ha_ref, hg_ref, s0a_ref, s0g_ref, s1a_ref, s1g_ref, cwa_ref, cwg_ref, cba_ref, cbg_ref, o_ref):
    def conv(h_ref, s0_ref, s1_ref, cw_ref, cb_ref):
        cw = cw_ref[...]
        return cb_ref[...] + s0_ref[...] * cw[0:1] + s1_ref[...] * cw[1:2] + h_ref[...] * cw[2:3]

    ca = conv(ha_ref, s0a_ref, s1a_ref, cwa_ref, cba_ref)
    cg = conv(hg_ref, s0g_ref, s1g_ref, cwg_ref, cbg_ref)
    o_ref[...] = (ca * jax.nn.gelu(cg)).astype(BF16)


def _dec_gate(h, s0, s1, conv_w, conv_b, *, tc=512):
    nb = h.shape[0]
    nj = D_FF // tc
    a_spec = pl.BlockSpec((nb, tc), lambda j: (0, j))
    g_spec = pl.BlockSpec((nb, tc), lambda j: (0, j + nj))
    return pl.pallas_call(
        _dec_gate_body,
        out_shape=jax.ShapeDtypeStruct((nb, D_FF), BF16),
        grid=(nj,),
        in_specs=[a_spec, g_spec, a_spec, g_spec, a_spec, g_spec,
                  pl.BlockSpec((3, tc), lambda j: (0, j)), pl.BlockSpec((3, tc), lambda j: (0, j + nj)),
                  pl.BlockSpec((1, tc), lambda j: (0, j)), pl.BlockSpec((1, tc), lambda j: (0, j + nj))],
        out_specs=pl.BlockSpec((nb, tc), lambda j: (0, j)),
        compiler_params=_cparams(("arbitrary",)),
        name="dec_gate",
    )(h, h, s0, s0, s1, s1, conv_w, conv_w, conv_b, conv_b)


def _rope_tables(pos, rot, hd):
    half = rot // 2
    inv = ROPE_THETA ** (-jnp.arange(half, dtype=F32) / half)
    ang = pos.astype(F32)[:, None] * inv[None, :]
    cos, sin = jnp.cos(ang), jnp.sin(ang)
    n = pos.shape[0]
    ones = jnp.ones((n, hd - rot), F32)
    zeros_r = jnp.zeros((n, hd - rot), F32)
    zeros_h = jnp.zeros((n, half), F32)
    c = jnp.concatenate([cos, cos, ones], axis=1)
    s1 = jnp.concatenate([zeros_h, sin, zeros_r], axis=1)
    s2 = jnp.concatenate([-sin, zeros_h, zeros_r], axis=1)
    reps = LANES // hd
    return tuple(jnp.tile(t, (1, reps)) for t in (c, s1, s2))


def _rope_body(x_ref, c_ref, s1_ref, s2_ref, o_ref, *, half):
    x = x_ref[...]
    w = x.shape[1]
    reps = w // LANES
    c = jnp.concatenate([c_ref[...]] * reps, axis=1)
    s1 = jnp.concatenate([s1_ref[...]] * reps, axis=1)
    s2 = jnp.concatenate([s2_ref[...]] * reps, axis=1)
    o_ref[...] = x * c + pltpu.roll(x, half, 1) * s1 + pltpu.roll(x, w - half, 1) * s2


def _rope(h, col0, width, tabs, *, half, tr, rows_per_seq):
    M = h.shape[0]
    cb = col0 // width
    nt = rows_per_seq // tr
    tab_spec = pl.BlockSpec((tr, LANES), lambda i: (i % nt, 0))
    return pl.pallas_call(
        functools.partial(_rope_body, half=half),
        out_shape=jax.ShapeDtypeStruct((M, width), F32),
        grid=(M // tr,),
        in_specs=[pl.BlockSpec((tr, width), lambda i: (i, cb)), tab_spec, tab_spec, tab_spec],
        out_specs=pl.BlockSpec((tr, width), lambda i: (i, 0)),
        compiler_params=_cparams(("parallel",)),
        name="rope",
    )(h, *tabs)


def _band_bias(tq, window, seq):
    w_al = -(-window // tq) * tq
    span = w_al + tq
    n_var = w_al // tq + 1
    r = jnp.arange(tq)[:, None]
    c = jnp.arange(span)[None, :]
    tabs = []
    for v in range(n_var):
        t0 = v * tq
        qpos = t0 + r
        kpos = max(t0 - w_al, 0) + c
        ok = (kpos <= qpos) & (qpos - kpos < window)
        tabs.append(jnp.where(ok, 0.0, NEG).astype(F32))
    return jnp.stack(tabs), w_al, span, n_var


def _stack_heads(qv, R, hd):
    return jnp.concatenate([qv[:, r * hd:(r + 1) * hd] for r in range(R)], axis=0)


def _unstack_heads(o, R, tq):
    return jnp.concatenate([o[r * tq:(r + 1) * tq] for r in range(R)], axis=1)


def _lane_max(s):
    m = s[:, :LANES]
    for c in range(1, s.shape[1] // LANES):
        m = jnp.maximum(m, s[:, c * LANES:(c + 1) * LANES])
    return m


def _row_max_lanes(mpart):
    return jnp.broadcast_to(jnp.max(mpart, axis=-1, keepdims=True), mpart.shape)


def _with_ones(v):
    return jnp.concatenate([v.astype(BF16), jnp.ones(v.shape, BF16)], axis=1)


def _win_body(q_ref, k_ref, v_ref, bias_ref, o_ref, *, tq, w_al, span):
    R, hd = NSA_GROUP, NSA_HD
    t0 = pl.program_id(2) * tq
    start = pl.multiple_of(jnp.maximum(t0 - w_al, 0), tq)
    q = _stack_heads(q_ref[...] * (hd ** -0.5), R, hd).astype(BF16)
    kb = k_ref[pl.ds(start, span), :].astype(BF16)
    vb = _with_ones(v_ref[pl.ds(start, span), :])
    s = lax.dot_general(q, kb, (((1,), (1,)), ((), ())), preferred_element_type=F32)
    s = (s.reshape(R, tq, span) + bias_ref[0][None]).reshape(R * tq, span)
    mb = _row_max_lanes(_lane_max(s))
    p = jnp.exp(s - jnp.concatenate([mb] * (span // LANES), axis=1))
    acc = jnp.dot(p.astype(BF16), vb, preferred_element_type=F32)
    o_ref[...] = _unstack_heads(acc[:, :hd] / acc[:, hd:], R, tq)


def _attn_win(q, k, v, v_cb, *, tq):
    R, hd, G = NSA_GROUP, NSA_HD, NSA_KV
    M = q.shape[0]
    nb = M // SEQ
    nq = SEQ // tq
    bias, w_al, span, n_var = _band_bias(tq, NSA_WINDOW, SEQ)
    return pl.pallas_call(
        functools.partial(_win_body, tq=tq, w_al=w_al, span=span),
        out_shape=jax.ShapeDtypeStruct((M, NSA_Q), F32),
        grid=(nb, G, nq),
        in_specs=[pl.BlockSpec((tq, R * hd), lambda b, g, i: (b * nq + i, g)),
                  pl.BlockSpec((SEQ, hd), lambda b, g, i: (b, g)),
                  pl.BlockSpec((SEQ, hd), lambda b, g, i: (b, v_cb + g)),
                  pl.BlockSpec((1, tq, span), lambda b, g, i: (jnp.minimum(i, n_var - 1), 0, 0))],
        out_specs=pl.BlockSpec((tq, R * hd), lambda b, g, i: (b * nq + i, g)),
        compiler_params=_cparams(("parallel", "parallel", "arbitrary")),
        name="attn_win",
    )(q, k, v, bias)


def _slc_body(q_ref, sel_ref, k_ref, v_ref, e_ref, tri_ref, o_ref, m_sc, acc_sc, *, tq):
    R, hd = NSA_GROUP, NSA_HD
    i = pl.program_id(2)
    t0 = pl.multiple_of(i * tq, tq)
    q = _stack_heads(q_ref[...] * (hd ** -0.5), R, hd)
    selm1 = sel_ref[0, 0] - 1.0
    qa = jnp.concatenate([q, jnp.concatenate([selm1] * R, axis=0)], axis=1).astype(BF16)

    def scores(k0):
        kb = jnp.concatenate([k_ref[pl.ds(k0, tq), :].astype(BF16), e_ref[pl.ds(k0, tq), :]], axis=1)
        return lax.dot_general(qa, kb, (((1,), (1,)), ((), ())), preferred_element_type=F32)

    s_diag = (scores(t0).reshape(R, tq, tq) + tri_ref[...][None]).reshape(R * tq, tq)

    m_sc[...] = _lane_max(s_diag)

    def max_step(kt, carry):
        m_sc[...] = jnp.maximum(m_sc[...], _lane_max(scores(pl.multiple_of(kt * tq, tq))))
        return carry

    lax.fori_loop(0, i, max_step, 0)
    mb = _row_max_lanes(m_sc[...])
    m_sc[...] = mb

    def weighted(s, k0):
        m2 = m_sc[...]
        p = jnp.exp(s - jnp.concatenate([m2] * (tq // LANES), axis=1))
        return jnp.dot(p.astype(BF16), _with_ones(v_ref[pl.ds(k0, tq), :]), preferred_element_type=F32)

    acc_sc[...] = weighted(s_diag, t0)

    def sum_step(kt, carry):
        k0 = pl.multiple_of(kt * tq, tq)
        acc_sc[...] += weighted(scores(k0), k0)
        return carry

    lax.fori_loop(0, i, sum_step, 0)
    acc = acc_sc[...]
    o_ref[...] = _unstack_heads(acc[:, :hd] / acc[:, hd:], R, tq)


def _attn_slc(q, k, v, v_cb, sel, *, tq):
    R, hd, G = NSA_GROUP, NSA_HD, NSA_KV
    M = q.shape[0]
    nb = M // SEQ
    nq = SEQ // tq
    kk = jnp.arange(SEQ)[:, None] // SLC_LEN
    e30 = jnp.where(kk == jnp.arange(LANES)[None, :], -NEG, 0.0).astype(BF16)
    r = jnp.arange(tq)
    tri = jnp.where(r[None, :] <= r[:, None], 0.0, NEG).astype(F32)
    return pl.pallas_call(
        functools.partial(_slc_body, tq=tq),
        out_shape=jax.ShapeDtypeStruct((M, NSA_Q), F32),
        grid=(nb, G, nq),
        in_specs=[pl.BlockSpec((tq, R * hd), lambda b, g, i: (b * nq + i, g)),
                  pl.BlockSpec((1, 1, tq, LANES), lambda b, g, i: (b, g, i, 0)),
                  pl.BlockSpec((SEQ, hd), lambda b, g, i: (b, g)),
                  pl.BlockSpec((SEQ, hd), lambda b, g, i: (b, v_cb + g)),
                  pl.BlockSpec((SEQ, LANES), lambda b, g, i: (0, 0)),
                  pl.BlockSpec((tq, tq), lambda b, g, i: (0, 0))],
        out_specs=pl.BlockSpec((tq, R * hd), lambda b, g, i: (b * nq + i, g)),
        scratch_shapes=[pltpu.VMEM((R * tq, LANES), F32), pltpu.VMEM((R * tq, 2 * hd), F32)],
        compiler_params=_cparams(("parallel", "parallel", "arbitrary")),
        name="attn_slc",
    )(q, sel, k, v, e30, tri)


def _swa_body(q_ref, k_ref, v_ref, bias_ref, sink_ref, o_ref, *, tq, w_al, span):
    npair = SWA_GROUP // 2
    t0 = pl.program_id(2) * tq
    start = pl.multiple_of(jnp.maximum(t0 - w_al, 0), tq)
    qv = q_ref[...] * (SWA_HD ** -0.5)
    qp = _stack_heads(qv, npair, LANES).astype(BF16)
    kf = k_ref[pl.ds(start, span), :]
    vf = v_ref[pl.ds(start, span), :]
    low_v = lax.broadcasted_iota(jnp.int32, (span, LANES), 1) < SWA_HD
    ke, ko = kf.astype(BF16), pltpu.roll(kf, SWA_HD, 1).astype(BF16)
    ve = jnp.where(low_v, vf, 1.0).astype(BF16)
    vo = jnp.where(low_v, 1.0, pltpu.roll(vf, SWA_HD, 1)).astype(BF16)
    dn = (((1,), (1,)), ((), ()))
    s = jnp.concatenate([lax.dot_general(qp, ke, dn, preferred_element_type=F32),
                         lax.dot_general(qp, ko, dn, preferred_element_type=F32)], axis=0)
    s = (s.reshape(SWA_GROUP, tq, span) + bias_ref[0][None]).reshape(SWA_GROUP * tq, span)
    sink = sink_ref[0]
    sink_b = jnp.concatenate([jnp.concatenate([sink[h]] * (tq // 8), axis=0) for h in range(SWA_GROUP)], axis=0)
    mb = jnp.maximum(_row_max_lanes(_lane_max(s)), sink_b)
    p = jnp.exp(s - jnp.concatenate([mb] * (span // LANES), axis=1)).astype(BF16)
    t_sink = jnp.exp(sink_b - mb)
    half = npair * tq
    o_e = jnp.dot(p[:half], ve, preferred_element_type=F32)
    o_o = jnp.dot(p[half:], vo, preferred_element_type=F32)
    low = lax.broadcasted_iota(jnp.int32, (half, LANES), 1) < SWA_HD
    numer = jnp.where(low, o_e, o_o)
    denom = pltpu.roll(jnp.where(low, o_o, o_e), SWA_HD, 1) + jnp.where(low, t_sink[:half], t_sink[half:])
    o_ref[...] = _unstack_heads(numer / denom, npair, tq)


def _attn_swa(q, k, v, v_cb, sinks, *, tq):
    G = SWA_KV
    M = q.shape[0]
    nb = M // SEQ
    nq = SEQ // tq
    bias, w_al, span, n_var = _band_bias(tq, SWA_WINDOW, SEQ)
    return pl.pallas_call(
        functools.partial(_swa_body, tq=tq, w_al=w_al, span=span),
        out_shape=jax.ShapeDtypeStruct((M, SWA_Q), F32),
        grid=(nb, G, nq),
        in_specs=[pl.BlockSpec((tq, SWA_GROUP * SWA_HD), lambda b, g, i: (b * nq + i, g)),
                  pl.BlockSpec((SEQ, LANES), lambda b, g, i: (b, g)),
                  pl.BlockSpec((SEQ, LANES), lambda b, g, i: (b, v_cb + g)),
                  pl.BlockSpec((1, tq, span), lambda b, g, i: (jnp.minimum(i, n_var - 1), 0, 0)),
                  pl.BlockSpec((1, SWA_GROUP, 8, LANES), lambda b, g, i: (g, 0, 0, 0))],
        out_specs=pl.BlockSpec((tq, SWA_GROUP * SWA_HD), lambda b, g, i: (b * nq + i, g)),
        compiler_params=_cparams(("parallel", "parallel", "arbitrary")),
        name="attn_swa",
    )(q, k, v, bias, sinks)


def _cmp_base(pe_ref, wc, b1_ref):
    base = (jnp.dot(pe_ref[0], wc[:, :CMP_HID], preferred_element_type=F32)
            + jnp.dot(pe_ref[1], wc[:, CMP_HID:], preferred_element_type=F32))
    return base[0:1] + b1_ref[...]


def _cmp_mlp_body(x_ref, wc_ref, pe_ref, b1_ref, w2_ref, o_ref, *, n_chunk):
    pieces = [x_ref[pl.ds(s, n_chunk, stride=CMP_STRIDE), :].astype(BF16) for s in range(CMP_STRIDE)]
    x = jnp.concatenate(pieces, axis=1)
    wc = wc_ref[0]
    p = jnp.dot(x, wc, preferred_element_type=F32)
    base = _cmp_base(pe_ref.at[0], wc, b1_ref.at[0])
    p1s = pltpu.roll(p[:, CMP_HID:], n_chunk - 1, 0)
    h = base + p[:, :CMP_HID] + p1s
    kb = jnp.dot(jax.nn.gelu(h).astype(BF16), w2_ref[0], preferred_element_type=F32)
    o_ref[0, 0, 0] = kb.astype(BF16)


def _cmp_mlp_prompt(h, wcat, pe2, b1, w2):
    M = h.shape[0]
    nb = M // SEQ
    n_chunk = SEQ // CMP_STRIDE
    cb0 = NSA_Q // NSA_HD
    return pl.pallas_call(
        functools.partial(_cmp_mlp_body, n_chunk=n_chunk),
        out_shape=jax.ShapeDtypeStruct((2, nb, NSA_KV, n_chunk, NSA_HD), BF16),
        grid=(2, nb, NSA_KV),
        in_specs=[pl.BlockSpec((SEQ, NSA_HD), lambda c, b, g: (b, cb0 + 4 * c + g)),
                  pl.BlockSpec((1, CMP_STRIDE * NSA_HD, 2 * CMP_HID), lambda c, b, g: (c, 0, 0)),
                  pl.BlockSpec((1, 2, 8, CMP_STRIDE * NSA_HD), lambda c, b, g: (c, 0, 0, 0)),
                  pl.BlockSpec((1, 1, CMP_HID), lambda c, b, g: (c, 0, 0)),
                  pl.BlockSpec((1, CMP_HID, NSA_HD), lambda c, b, g: (c, 0, 0))],
        out_specs=pl.BlockSpec((1, 1, 1, n_chunk, NSA_HD), lambda c, b, g: (c, b, g, 0, 0)),
        compiler_params=_cparams(("parallel", "parallel", "parallel")),
        name="cmp_mlp",
    )(h, wcat, pe2, b1, w2)


def _cmp_attn_body(q_ref, kb_ref, vb_ref, ov_ref, o_ref, sel_ref, *, tq, n_cmp, n_slc):
    R, hd = NSA_GROUP, NSA_HD
    i = pl.program_id(2)
    t0 = i * tq
    qv = q_ref[...]
    q = jnp.concatenate([qv[:, r * hd:(r + 1) * hd] for r in range(R)], axis=0).astype(BF16)
    s = lax.dot_general(q, kb_ref[0, 0, 0], (((1,), (1,)), ((), ())), preferred_element_type=F32) * (hd ** -0.5)
    n = lax.broadcasted_iota(jnp.int32, (tq, LANES), 1)
    qpos = t0 + lax.broadcasted_iota(jnp.int32, (tq, LANES), 0)
    mask = ((n * CMP_STRIDE + CMP_LEN - 1) <= qpos) & (n < n_cmp)
    mask3 = mask[None]
    s3 = jnp.where(mask3, s.reshape(R, tq, LANES), NEG)
    m = jnp.max(s3, axis=-1, keepdims=True)
    e = jnp.where(mask3, jnp.exp(s3 - m), 0.0)
    l = jnp.sum(e, axis=-1, keepdims=True)
    p = e / jnp.where(l > 0.0, l, 1.0)
    o = jnp.dot(p.reshape(R * tq, LANES).astype(BF16), vb_ref[0, 0, 0], preferred_element_type=F32)
    o_ref[...] = jnp.concatenate([o[r * tq:(r + 1) * tq] for r in range(R)], axis=1)

    psum = jnp.sum(p, axis=0)
    imp = jnp.dot(psum.astype(BF16), ov_ref[...], preferred_element_type=F32)
    j = n
    cur = qpos // SLC_LEN
    forced = (j == 0) | (j == cur) | (j == cur - 1)
    imp = jnp.where(forced, FORCE_SCORE, jnp.where(j <= cur, imp, -1.0))
    it = imp.T[0:n_slc]
    jr = lax.broadcasted_iota(jnp.int32, (n_slc, tq), 0)
    cnt = jnp.zeros((n_slc, tq), F32)
    for c in range(n_slc):
        ci = it[c:c + 1, :]
        before = (ci > it) | ((ci == it) & (jr > c))
        cnt = cnt + before.astype(F32)
    sel_t = (cnt < float(SLC_TOPK)).astype(F32)
    sel_ref[0, 0] = jnp.concatenate([sel_t, jnp.zeros((LANES - n_slc, tq), F32)], axis=0).T


def _overlap_matrix(n_rows, n_cols, n_cmp, n_slc):
    i = jnp.arange(n_rows)[:, None] * CMP_STRIDE
    j = jnp.arange(n_cols)[None, :] * SLC_LEN
    ok = (i < j + SLC_LEN) & (i + CMP_LEN > j) & (jnp.arange(n_rows)[:, None] < n_cmp) & (jnp.arange(n_cols)[None, :] < n_slc)
    return ok.astype(BF16)


def _cmp_attn_prompt(h, kvb, *, tq):
    M = h.shape[0]
    nb = M // SEQ
    nq = SEQ // tq
    n_cmp = SEQ // CMP_STRIDE - 1
    n_slc = SEQ // SLC_LEN
    ov = _overlap_matrix(LANES, LANES, n_cmp, n_slc)
    return pl.pallas_call(
        functools.partial(_cmp_attn_body, tq=tq, n_cmp=n_cmp, n_slc=n_slc),
        out_shape=(jax.ShapeDtypeStruct((M, NSA_Q), F32),
                   jax.ShapeDtypeStruct((nb, NSA_KV, SEQ, LANES), F32)),
        grid=(nb, NSA_KV, nq),
        in_specs=[pl.BlockSpec((tq, NSA_GROUP * NSA_HD), lambda b, g, i: (b * nq + i, g)),
                  pl.BlockSpec((1, 1, 1, LANES, NSA_HD), lambda b, g, i: (0, b, g, 0, 0)),
                  pl.BlockSpec((1, 1, 1, LANES, NSA_HD), lambda b, g, i: (1, b, g, 0, 0)),
                  pl.BlockSpec((LANES, LANES), lambda b, g, i: (0, 0))],
        out_specs=(pl.BlockSpec((tq, NSA_GROUP * NSA_HD), lambda b, g, i: (b * nq + i, g)),
                   pl.BlockSpec((1, 1, tq, LANES), lambda b, g, i: (b, g, i, 0))),
        compiler_params=_cparams(("parallel", "parallel", "arbitrary")),
        name="cmp_attn",
    )(h, kvb, kvb, ov)


def _dec_blocks_body(pt_ref, pool_ref, wc_ref, pe_ref, b1_ref, w2_ref, new_ref, o_ref, buf, sem, carry, *, layer):
    G = NSA_KV
    P = PAGES_PER_STEP
    mrows = 8 * P
    slab = mrows * CHUNK_PITCH
    b = pl.program_id(0)
    j = pl.program_id(1)
    nj = pl.num_programs(1)
    slot = j % 2

    nb = pl.num_programs(0)

    def page_copy(bb, jj, sl, p):
        phys = pt_ref[bb, jj * P + p]
        return pltpu.make_async_copy(pool_ref.at[layer, phys],
                                     buf.at[sl, pl.ds(8 * p, 8), pl.ds(0, 64), :], sem.at[sl])

    @pl.when((b == 0) & (j == 0))
    def _():
        for p in range(P):
            page_copy(0, 0, 0, p).start()

    @pl.when(j + 1 < nj)
    def _():
        for p in range(P):
            page_copy(b, j + 1, 1 - slot, p).start()

    @pl.when((j + 1 == nj) & (b + 1 < nb))
    def _():
        for p in range(P):
            page_copy(b + 1, 0, 1 - slot, p).start()

    for p in range(P):
        page_copy(b, j, slot, p).wait()

    flat = buf.reshape(2 * slab, LANES)
    base_row = slot * slab
    wc = wc_ref[...]
    w2 = w2_ref[...]
    ks = 4
    pr = None
    for s0 in range(0, CMP_STRIDE, ks):
        xs = []
        for g in range(G):
            pieces = [flat[pl.ds(base_row + 4 * s + g, mrows, stride=CHUNK_PITCH), :].astype(BF16)
                      for s in range(s0, s0 + ks)]
            xs.append(jnp.concatenate(pieces, axis=1))
        xk = jnp.concatenate(xs, axis=0)
        d = jnp.dot(xk, wc[s0 * NSA_HD:(s0 + ks) * NSA_HD, :], preferred_element_type=F32)
        pr = d if pr is None else pr + d
    base = _cmp_base(pe_ref, wc, b1_ref)
    p0 = pr[:, :CMP_HID]
    p1 = pr[:, CMP_HID:]
    p1s = pltpu.roll(p1, G * mrows - 1, 0)
    kb = jnp.dot(jax.nn.gelu(base + p0 + p1s).astype(BF16), w2, preferred_element_type=F32)
    row0 = pl.multiple_of(j * mrows, mrows)
    for g in range(G):
        o_ref[0, g, pl.ds(row0, mrows), :] = kb[g * mrows:(g + 1) * mrows]

    zero4 = jnp.zeros((8 - G, CMP_HID), F32)

    def fix_rows(a, c):
        hfix = base + a + c
        return jnp.dot(jax.nn.gelu(hfix).astype(BF16), w2, preferred_element_type=F32)

    @pl.when(j > 0)
    def _():
        first = jnp.concatenate([p1[g * mrows:g * mrows + 1] for g in range(G)] + [zero4], axis=0)
        kf = fix_rows(carry[...], first)
        for g in range(G):
            o_ref[0, g, pl.ds(row0 - 1, 1), :] = kf[g:g + 1]

    lastp0 = jnp.concatenate([p0[(g + 1) * mrows - 1:(g + 1) * mrows] for g in range(G)] + [zero4], axis=0)
    carry[...] = lastp0

    @pl.when(j == nj - 1)
    def _():
        pn = jnp.dot(new_ref[0].astype(BF16), wc[0:NSA_HD, CMP_HID:], preferred_element_type=F32)
        kf = fix_rows(lastp0, pn)
        for g in range(G):
            o_ref[0, g, pl.ds(row0 + mrows - 1, 1), :] = kf[g:g + 1]


def _dec_blocks(page_table, pool5, layer, wcat, pe2, b1, w2, new_rows):
    nb = page_table.shape[0]
    P = PAGES_PER_STEP
    nj = N_PAGES // P
    assert nj % 2 == 0, "the double-buffer slot of a step must not depend on the sample"
    grid_spec = pltpu.PrefetchScalarGridSpec(
        num_scalar_prefetch=1,
        grid=(nb, nj),
        in_specs=[pl.BlockSpec(memory_space=pl.ANY),
                  pl.BlockSpec((CMP_STRIDE * NSA_HD, 2 * CMP_HID), lambda b, j, pt: (0, 0)),
                  pl.BlockSpec((2, 8, CMP_STRIDE * NSA_HD), lambda b, j, pt: (0, 0, 0)),
                  pl.BlockSpec((1, CMP_HID), lambda b, j, pt: (0, 0)),
                  pl.BlockSpec((CMP_HID, NSA_HD), lambda b, j, pt: (0, 0)),
                  pl.BlockSpec((1, 8, NSA_HD), lambda b, j, pt: (b, 0, 0))],
        out_specs=pl.BlockSpec((1, NSA_KV, N_PAST_CHUNK, NSA_HD), lambda b, j, pt: (b, 0, 0, 0)),
        scratch_shapes=[pltpu.VMEM((2, 8 * P, CHUNK_PITCH, LANES), F32),
                        pltpu.SemaphoreType.DMA((2,)),
                        pltpu.VMEM((8, CMP_HID), F32)],
    )
    return pl.pallas_call(
        functools.partial(_dec_blocks_body, layer=layer),
        out_shape=jax.ShapeDtypeStruct((nb, NSA_KV, N_PAST_CHUNK, NSA_HD), F32),
        grid_spec=grid_spec,
        compiler_params=_cparams(("arbitrary", "arbitrary")),
        name="dec_blocks",
    )(page_table, pool5, wcat, pe2, b1, w2, new_rows)


def _dec_cmp_attn_body(q_ref, kb_ref, vb_ref, o_ref, idx_ref):
    G, R, hd = NSA_KV, NSA_GROUP, NSA_HD
    n_blk = N_PAST_CHUNK
    nn = lax.broadcasted_iota(jnp.int32, (n_blk, SLC_LANES), 0) * CMP_STRIDE
    jj = lax.broadcasted_iota(jnp.int32, (n_blk, SLC_LANES), 1)
    ov = ((nn < jj * SLC_LEN + SLC_LEN) & (nn + CMP_LEN > jj * SLC_LEN) & (jj < N_SLC_DEC)).astype(BF16)
    n = lax.broadcasted_iota(jnp.int32, (8, n_blk), 1)
    mask = (n * CMP_STRIDE + CMP_LEN - 1) <= PAST_LEN
    rowi = lax.broadcasted_iota(jnp.int32, (8, n_blk), 0)
    imps = []
    for g in range(G):
        q = q_ref[0, g].astype(BF16)
        s = lax.dot_general(q, kb_ref[0, g].astype(BF16), (((1,), (1,)), ((), ())),
                            preferred_element_type=F32) * (hd ** -0.5)
        s = jnp.where(mask, s, NEG)
        m = jnp.max(s, axis=-1, keepdims=True)
        e = jnp.where(mask, jnp.exp(s - m), 0.0)
        l = jnp.sum(e, axis=-1, keepdims=True)
        p = e / jnp.where(l > 0.0, l, 1.0)
        o_ref[0, g] = jnp.dot(p.astype(BF16), vb_ref[0, g].astype(BF16), preferred_element_type=F32)
        psum = jnp.sum(jnp.where(rowi < R, p, 0.0), axis=0, keepdims=True)
        imp8 = jnp.dot(jnp.broadcast_to(psum, (8, n_blk)).astype(BF16), ov, preferred_element_type=F32)
        imps.append(imp8[0:1])
    imp = jnp.concatenate(imps + [jnp.zeros((8 - G, SLC_LANES), F32)], axis=0)
    j = lax.broadcasted_iota(jnp.int32, (8, SLC_LANES), 1)
    cur = PAST_LEN // SLC_LEN
    forced = (j == 0) | (j == cur) | (j == cur - 1)
    imp = jnp.where(forced, FORCE_SCORE, jnp.where(j <= cur, imp, -1.0))
    imp = jnp.where(j < N_SLC_DEC, imp, -jnp.inf)
    lane = lax.broadcasted_iota(jnp.int32, (8, LANES), 1)
    jf = j.astype(F32)
    out = jnp.zeros((8, LANES), F32)
    for k in range(SLC_TOPK):
        mx = jnp.max(imp, axis=-1, keepdims=True)
        pick = jnp.min(jnp.where(imp == mx, jf, float(SLC_LANES)), axis=-1, keepdims=True)
        out = jnp.where(lane == k, pick, out)
        imp = jnp.where(jf == pick, -jnp.inf, imp)
    idx_ref[0] = out.astype(jnp.int32)


def _dec_cmp_attn(q, kb, vb):
    nb = q.shape[0]
    blk = pl.BlockSpec((1, NSA_KV, N_PAST_CHUNK, NSA_HD), lambda b: (b, 0, 0, 0))
    qspec = pl.BlockSpec((1, NSA_KV, 8, NSA_HD), lambda b: (b, 0, 0, 0))
    return pl.pallas_call(
        _dec_cmp_attn_body,
        out_shape=(jax.ShapeDtypeStruct((nb, NSA_KV, 8, NSA_HD), F32),
                   jax.ShapeDtypeStruct((nb, 8, LANES), jnp.int32)),
        grid=(nb,),
        in_specs=[qspec, blk, blk],
        out_specs=(qspec, pl.BlockSpec((1, 8, LANES), lambda b: (b, 0, 0))),
        compiler_params=_cparams(("parallel",)),
        name="dec_cmp_attn",
    )(q, kb, vb)


def _dec_slc_body(idx_ref, pt_ref, q_ref, kn_ref, vn_ref, pk_ref, pv_ref, o_ref, kbuf, vbuf, sem, *, layer):
    G, R, hd = NSA_KV, NSA_GROUP, NSA_HD
    K = SLC_TOPK
    b = pl.program_id(0)

    def copies(g, k):
        jsel = idx_ref[(b * G + g) * K + k]
        jp = jnp.minimum(jsel, N_PAST_BLK - 1)
        phys = pt_ref[b, lax.shift_right_logical(jp, 1)]
        rows = pl.ds((jp & 1) * SLC_LEN, SLC_LEN)
        dst = pl.ds((g * K + k) * SLC_LEN, SLC_LEN)
        ck = pltpu.make_async_copy(pk_ref.at[layer, phys, rows, g, :], kbuf.at[dst, :], sem.at[0])
        cv = pltpu.make_async_copy(pv_ref.at[layer, phys, rows, g, :], vbuf.at[dst, :], sem.at[1])
        return ck, cv

    for g in range(G):
        for k in range(K):
            ck, cv = copies(g, k)
            ck.start()
            cv.start()
    for g in range(G):
        for k in range(K):
            ck, cv = copies(g, k)
            ck.wait()
            cv.wait()

    nkeys = K * SLC_LEN
    slot = lax.broadcasted_iota(jnp.int32, (8, nkeys), 1) // SLC_LEN
    scale = hd ** -0.5
    for g in range(G):
        valid = jnp.zeros((8, nkeys), jnp.int32)
        n_new_i = jnp.int32(0)
        for k in range(K):
            jsel = idx_ref[(b * G + g) * K + k]
            past_i = jnp.where(jsel < N_PAST_BLK, 1, 0).astype(jnp.int32)
            valid = jnp.where(slot == k, past_i, valid)
            n_new_i = n_new_i + (1 - past_i)
        mask = valid > 0
        n_new = jnp.full((8, 1), n_new_i, jnp.int32).astype(F32)
        q = q_ref[0, g].astype(BF16)
        kg = kbuf[g * nkeys:(g + 1) * nkeys, :].astype(BF16)
        vg = vbuf[g * nkeys:(g + 1) * nkeys, :].astype(BF16)
        s = lax.dot_general(q, kg, (((1,), (1,)), ((), ())), preferred_element_type=F32) * scale
        s = jnp.where(mask, s, NEG)
        kn = kn_ref[0, g][0:1].astype(BF16).astype(F32)
        vn = vn_ref[0, g][0:1].astype(BF16).astype(F32)
        has_new = n_new > 0.5
        s_new = jnp.sum(q.astype(F32) * kn, axis=-1, keepdims=True) * scale
        s_new = jnp.where(has_new, s_new, NEG)
        m = jnp.maximum(jnp.max(s, axis=-1, keepdims=True), s_new)
        e = jnp.where(mask, jnp.exp(s - m), 0.0)
        e_new = jnp.where(has_new, jnp.exp(s_new - m), 0.0) * n_new
        l = jnp.sum(e, axis=-1, keepdims=True) + e_new
        pv = jnp.dot(e.astype(BF16), vg, preferred_element_type=F32) + e_new * vn
        o_ref[0, g] = pv / l


def _dec_slc(idx_flat, page_table, q, kn, vn, pool_k, pool_v, layer):
    nb = q.shape[0]
    qspec = pl.BlockSpec((1, NSA_KV, 8, NSA_HD), lambda b, idx, pt: (b, 0, 0, 0))
    nrow = NSA_KV * SLC_TOPK * SLC_LEN
    grid_spec = pltpu.PrefetchScalarGridSpec(
        num_scalar_prefetch=2,
        grid=(nb,),
        in_specs=[qspec, qspec, qspec, pl.BlockSpec(memory_space=pl.ANY), pl.BlockSpec(memory_space=pl.ANY)],
        out_specs=qspec,
        scratch_shapes=[pltpu.VMEM((nrow, NSA_HD), F32), pltpu.VMEM((nrow, NSA_HD), F32),
                        pltpu.SemaphoreType.DMA((2,))],
    )
    return pl.pallas_call(
        functools.partial(_dec_slc_body, layer=layer),
        out_shape=jax.ShapeDtypeStruct((nb, NSA_KV, 8, NSA_HD), F32),
        grid_spec=grid_spec,
        compiler_params=_cparams(("arbitrary",)),
        name="dec_slc",
    )(idx_flat, page_table, q, kn, vn, pool_k, pool_v)


def _dec_win_body(*refs, G, R, hd, L, window, use_sink):
    if use_sink:
        q_ref, kb_ref, vb_ref, kn_ref, vn_ref, sink_ref, o_ref = refs
    else:
        q_ref, kb_ref, vb_ref, kn_ref, vn_ref, o_ref = refs
    scale = hd ** -0.5
    i = lax.broadcasted_iota(jnp.int32, (8, L), 1)
    mask = (L - i) < window
    for g in range(G):
        q = q_ref[0, g].astype(BF16)
        s = lax.dot_general(q, kb_ref[0, g].astype(BF16), (((1,), (1,)), ((), ())),
                            preferred_element_type=F32) * scale
        s = jnp.where(mask, s, NEG)
        kn = kn_ref[0, g][0:1].astype(BF16).astype(F32)
        vn = vn_ref[0, g][0:1].astype(BF16).astype(F32)
        s_new = jnp.sum(q.astype(F32) * kn, axis=-1, keepdims=True) * scale
        m = jnp.maximum(jnp.max(s, axis=-1, keepdims=True), s_new)
        if use_sink:
            sk = sink_ref[g][:, 0:1]
            m = jnp.maximum(m, sk)
        e = jnp.where(mask, jnp.exp(s - m), 0.0)
        e_new = jnp.exp(s_new - m)
        l = jnp.sum(e, axis=-1, keepdims=True) + e_new
        if use_sink:
            l = l + jnp.exp(sk - m)
        pv = jnp.dot(e.astype(BF16), vb_ref[0, g].astype(BF16), preferred_element_type=F32) + e_new * vn
        o_ref[0, g] = pv / l


def _dec_win(q, kb, vb, kn, vn, sinks, *, window):
    nb, G, _, hd = q.shape
    L = kb.shape[2]
    R = 8
    qspec = pl.BlockSpec((1, G, 8, hd), lambda b: (b, 0, 0, 0))
    bspec = pl.BlockSpec((1, G, L, hd), lambda b: (b, 0, 0, 0))
    in_specs = [qspec, bspec, bspec, qspec, qspec]
    args = [q, kb, vb, kn, vn]
    if sinks is not None:
        in_specs.append(pl.BlockSpec((G, 8, LANES), lambda b: (0, 0, 0)))
        args.append(sinks)
    return pl.pallas_call(
        functools.partial(_dec_win_body, G=G, R=R, hd=hd, L=L, window=window, use_sink=sinks is not None),
        out_shape=jax.ShapeDtypeStruct((nb, G, 8, hd), F32),
        grid=(nb,),
        in_specs=in_specs,
        out_specs=qspec,
        compiler_params=_cparams(("parallel",)),
        name="dec_win",
    )(*args)


def _cmp_params(pe, w1, b1, w2):
    wcat = jnp.transpose(w1.reshape(2, CMP_STRIDE * NSA_HD, CMP_HID), (1, 0, 2)).reshape(CMP_STRIDE * NSA_HD, 2 * CMP_HID)
    pe2 = jnp.broadcast_to(pe.reshape(2, 1, CMP_STRIDE * NSA_HD), (2, 8, CMP_STRIDE * NSA_HD))
    return wcat.astype(BF16), pe2.astype(BF16), b1.reshape(1, CMP_HID), w2.astype(BF16)


def _group_rows(x, G, R, hd):
    nb = x.shape[0]
    x = x.reshape(nb, G, R, hd)
    if R < 8:
        x = jnp.pad(x, ((0, 0), (0, 0), (0, 8 - R), (0, 0)))
    return x


def _nsa_layer(a, xp, xb, xs, xsb, caches, page_table, win_k, win_v, w_main, w_gate, w_o, cmpk, cmpv, ln_g, ln_b,
               tabs_p, tabs_s):
    pool_ck, pool_cv, pool_sk, pool_sv = caches
    nb = DEC_BATCH
    half = NSA_ROT // 2
    h = _mm(xb, w_main, tm=1024, tn=512)
    hg = _mm(xb, w_gate, tm=1024, tn=LANES)
    wcat = jnp.stack([cmpk[0], cmpv[0]])
    pe2 = jnp.stack([cmpk[1], cmpv[1]])
    b1 = jnp.stack([cmpk[2], cmpv[2]])
    w2 = jnp.stack([cmpk[3], cmpv[3]])
    kvb = _cmp_mlp_prompt(h, wcat, pe2, b1, w2)
    o_cmp, sel = _cmp_attn_prompt(h, kvb, tq=256)
    q_rot = _rope(h, 0, NSA_Q, tabs_p, half=half, tr=512, rows_per_seq=SEQ)
    ks_rot = _rope(h, NSA_Q + 2 * NSA_KVD, NSA_KVD, tabs_p, half=half, tr=512, rows_per_seq=SEQ)
    kw_rot = _rope(h, NSA_Q + 4 * NSA_KVD, NSA_KVD, tabs_p, half=half, tr=512, rows_per_seq=SEQ)
    vs_cb = (NSA_Q + 3 * NSA_KVD) // NSA_HD
    vw_cb = (NSA_Q + 5 * NSA_KVD) // NSA_HD
    o_slc = _attn_slc(q_rot, ks_rot, h, vs_cb, sel, tq=256)
    o_win = _attn_win(q_rot, kw_rot, h, vw_cb, tq=256)
    yp, ypb = _out_ln([o_cmp, o_slc, o_win], hg, w_o, xp, ln_g, ln_b, tm=256, tk=NSA_Q)

    def kv5(arr):
        return arr.reshape(BATCH, SEQ, NSA_KV, NSA_HD)

    def hs(i):
        return h[:, NSA_Q + i * NSA_KVD:NSA_Q + (i + 1) * NSA_KVD]

    lw = min(NSA_WINDOW, SEQ)
    st_p = (kv5(hs(0)), kv5(hs(1)), kv5(ks_rot), kv5(hs(3)), kv5(kw_rot)[:, SEQ - lw:], kv5(hs(5))[:, SEQ - lw:])

    hd_ = _mm(xsb, w_main, tm=nb, tn=512)
    hgd = _mm(xsb, w_gate, tm=nb, tn=LANES)
    qd_rot = _rope(hd_, 0, NSA_Q, tabs_s, half=half, tr=nb, rows_per_seq=nb)
    ksd_rot = _rope(hd_, NSA_Q + 2 * NSA_KVD, NSA_KVD, tabs_s, half=half, tr=nb, rows_per_seq=nb)
    kwd_rot = _rope(hd_, NSA_Q + 4 * NSA_KVD, NSA_KVD, tabs_s, half=half, tr=nb, rows_per_seq=nb)

    def hsd(i):
        return hd_[:, NSA_Q + i * NSA_KVD:NSA_Q + (i + 1) * NSA_KVD]

    kc_new, vc_new, vs_new, vw_new = hsd(0), hsd(1), hsd(3), hsd(5)
    def rows8(x):
        return jnp.pad(x.reshape(nb, NSA_KV, NSA_HD), ((0, 0), (0, 8 - NSA_KV), (0, 0)))

    kblk = _dec_blocks(page_table, pool_ck, a, cmpk[0], cmpk[1], cmpk[2], cmpk[3], rows8(kc_new))
    vblk = _dec_blocks(page_table, pool_cv, a, cmpv[0], cmpv[1], cmpv[2], cmpv[3], rows8(vc_new))
    qd = _group_rows(hd_[:, :NSA_Q], NSA_KV, NSA_GROUP, NSA_HD)
    qd_r = _group_rows(qd_rot, NSA_KV, NSA_GROUP, NSA_HD)
    od_cmp, idx = _dec_cmp_attn(qd, kblk, vblk)
    idx_flat = idx[:, :NSA_KV, :SLC_TOPK].reshape(-1)
    od_slc = _dec_slc(idx_flat, page_table, qd_r, _group_rows(ksd_rot, NSA_KV, 1, NSA_HD),
                      _group_rows(vs_new, NSA_KV, 1, NSA_HD), pool_sk, pool_sv, a)
    wk_hm = jnp.transpose(win_k, (0, 2, 1, 3))
    wv_hm = jnp.transpose(win_v, (0, 2, 1, 3))
    od_win = _dec_win(qd_r, wk_hm, wv_hm, _group_rows(kwd_rot, NSA_KV, 1, NSA_HD),
                      _group_rows(vw_new, NSA_KV, 1, NSA_HD), None, window=NSA_WINDOW)

    def flat_heads(o):
        return o[:, :, :NSA_GROUP, :].reshape(nb, NSA_Q)

    ys, ysb = _out_ln([flat_heads(od_cmp), flat_heads(od_slc), flat_heads(od_win)], hgd, w_o, xs, ln_g, ln_b,
                      tm=nb, tk=NSA_Q)

    def new5(arr):
        return arr.reshape(nb, 1, NSA_KV, NSA_HD)

    st_s = (new5(kc_new), new5(vc_new), new5(ksd_rot), new5(vs_new),
            jnp.concatenate([win_k[:, 1:], new5(kwd_rot)], axis=1),
            jnp.concatenate([win_v[:, 1:], new5(vw_new)], axis=1))
    return yp, ypb, ys, ysb, st_p, st_s


def _swa_layer(xp, xb, xs, xsb, buf_k, buf_v, w_in, b_in, w_pad, b_pad, sinks, w_o, ln_g, ln_b, tabs_p, tabs_pk, tabs_s):
    nb = DEC_BATCH
    half = SWA_ROT // 2
    G, R, hd = SWA_KV, SWA_GROUP, SWA_HD
    kpad = G * LANES
    h = _mm(xb, w_pad, b_pad, tm=1024, tn=512)
    q_rot = _rope(h, 0, SWA_Q, tabs_p, half=half, tr=512, rows_per_seq=SEQ)
    k_rot = _rope(h, SWA_Q, kpad, tabs_pk, half=half, tr=512, rows_per_seq=SEQ)
    sink_p = jnp.transpose(sinks.reshape(G, R // 2, 2), (0, 2, 1)).reshape(G, R, 1, 1)
    sink_p = jnp.broadcast_to(sink_p, (G, R, 8, LANES))
    o = _attn_swa(q_rot, k_rot, h, (SWA_Q + kpad) // LANES, sink_p, tq=128)
    yp, ypb = _out_ln([o], None, w_o, xp, ln_g, ln_b, tm=512, tk=SWA_Q)
    lw = min(SWA_WINDOW, SEQ)
    v_pad = h[:, SWA_Q + kpad:]
    st_p = (k_rot.reshape(BATCH, SEQ, G, LANES)[:, SEQ - lw:, :, :hd],
            v_pad.reshape(BATCH, SEQ, G, LANES)[:, SEQ - lw:, :, :hd])

    hd_ = _mm(xsb, w_in, b_in, tm=nb, tn=512)
    qd_rot = _rope(hd_, 0, SWA_Q, tabs_s, half=half, tr=nb, rows_per_seq=nb)
    kd_rot = _rope(hd_, SWA_Q, SWA_KVD, tabs_s, half=half, tr=nb, rows_per_seq=nb)
    vd = hd_[:, SWA_Q + SWA_KVD:]
    sink_s = jnp.broadcast_to(sinks.reshape(G, R, 1), (G, R, LANES))
    od = _dec_win(_group_rows(qd_rot, G, R, hd), jnp.transpose(buf_k, (0, 2, 1, 3)), jnp.transpose(buf_v, (0, 2, 1, 3)),
                  _group_rows(kd_rot, G, 1, hd), _group_rows(vd, G, 1, hd), sink_s, window=SWA_WINDOW)
    ys, ysb = _out_ln([od.reshape(nb, SWA_Q)], None, w_o, xs, ln_g, ln_b, tm=nb, tk=SWA_Q)
    kn5 = kd_rot.reshape(nb, 1, G, hd)
    vn5 = vd.reshape(nb, 1, G, hd)
    st_s = (jnp.concatenate([buf_k[:, 1:], kn5], axis=1), jnp.concatenate([buf_v[:, 1:], vn5], axis=1))
    return yp, ypb, ys, ysb, st_p, st_s


def _ffn_layer(layer, xp, xb, xs, xsb, state, w_up, conv_w, conv_b, w_down, ln_g, ln_b):
    nb = DEC_BATCH
    yp, ypb, sa, sg = _ffn_prompt(xb, xp, w_up, conv_w, conv_b, w_down, ln_g, ln_b, layer=layer)
    tps = sa.shape[0] // BATCH
    conv_p = jnp.concatenate([sa[tps - 1::tps, 6:8], sg[tps - 1::tps, 6:8]], axis=-1)
    hup = _mm(xsb, w_up, tm=nb, tn=512, layer=layer)
    gated = _dec_gate(hup, state[:, 0], state[:, 1], conv_w, conv_b)
    ys, ysb = _out_ln([gated], None, w_down, xs, ln_g, ln_b, tm=nb, tk=512, layer=layer)
    conv_s = jnp.concatenate([state[:, 1:], hup[:, None, :]], axis=1)
    return yp, ypb, ys, ysb, conv_p, conv_s


def _swa_padded(w_in, b_in):
    def widen(t):
        lead = t.shape[:-1]
        t = t.reshape(lead + (SWA_KV, SWA_HD))
        t = jnp.pad(t, [(0, 0)] * len(lead) + [(0, 0), (0, LANES - SWA_HD)])
        return t.reshape(lead + (SWA_KV * LANES,))

    def cols(t):
        return jnp.concatenate([t[..., :SWA_Q], widen(t[..., SWA_Q:SWA_Q + SWA_KVD]), widen(t[..., SWA_Q + SWA_KVD:])], axis=-1)

    return cols(w_in), cols(b_in)


def kernel(x_prompt, x_sample, cache_nsa_cmp_k, cache_nsa_cmp_v, cache_nsa_slc_k, cache_nsa_slc_v, state_nsa_win_k, state_nsa_win_v, state_swa_k, state_swa_v, state_conv, page_table, nsa_w_in, nsa_w_o, nsa_cmp_pe_k, nsa_cmp_w1_k, nsa_cmp_b1_k, nsa_cmp_w2_k, nsa_cmp_pe_v, nsa_cmp_w1_v, nsa_cmp_b1_v, nsa_cmp_w2_v, swa_w_in, swa_b_in, swa_sinks, swa_w_o, ln1_g, ln1_b, ln2_g, ln2_b, ffn_w_up, ffn_conv_w, ffn_conv_b, ffn_w_down):
    n_nsa, n_pool = cache_nsa_cmp_k.shape[:2]
    pool_ck = cache_nsa_cmp_k.reshape(n_nsa, n_pool, 8, 64, NSA_HD)
    pool_cv = cache_nsa_cmp_v.reshape(n_nsa, n_pool, 8, 64, NSA_HD)
    pool_sk = cache_nsa_slc_k
    pool_sv = cache_nsa_slc_v

    pos_p = jnp.arange(SEQ)
    pos_s = jnp.full((DEC_BATCH,), PAST_LEN)
    nsa_tabs_p = _rope_tables(pos_p, NSA_ROT, NSA_HD)
    nsa_tabs_s = _rope_tables(pos_s, NSA_ROT, NSA_HD)
    swa_tabs_p = _rope_tables(pos_p, SWA_ROT, SWA_HD)
    swa_tabs_s = _rope_tables(pos_s, SWA_ROT, SWA_HD)
    swa_tabs_pk = _rope_tables(pos_p, SWA_ROT, LANES)

    yp = x_prompt.reshape(BATCH * SEQ, D_MODEL)
    ys = x_sample.reshape(DEC_BATCH, D_MODEL)
    ypb, ysb = yp.astype(BF16), ys.astype(BF16)
    w_up_bf, w_down_bf = ffn_w_up.astype(BF16), ffn_w_down.astype(BF16)
    nsa_p, nsa_s, swa_p, swa_s, conv_p, conv_s = [], [], [], [], [], []
    for i in range(DEPTH):
        a = i // 2
        g1, b1 = ln1_g[i].reshape(1, D_MODEL), ln1_b[i].reshape(1, D_MODEL)
        g2, b2 = ln2_g[i].reshape(1, D_MODEL), ln2_b[i].reshape(1, D_MODEL)
        if i % 2 == 0:
            w_in = nsa_w_in[a]
            w_main = w_in[:, :NSA_MAIN].astype(BF16)
            w_gate = jnp.pad(w_in[:, NSA_MAIN:], ((0, 0), (0, LANES - 3 * NSA_HEADS))).astype(BF16)
            cmpk = _cmp_params(nsa_cmp_pe_k[a], nsa_cmp_w1_k[a], nsa_cmp_b1_k[a], nsa_cmp_w2_k[a])
            cmpv = _cmp_params(nsa_cmp_pe_v[a], nsa_cmp_w1_v[a], nsa_cmp_b1_v[a], nsa_cmp_w2_v[a])
            yp, ypb, ys, ysb, st_p, st_s = _nsa_layer(
                a, yp, ypb, ys, ysb, (pool_ck, pool_cv, pool_sk, pool_sv), page_table,
                state_nsa_win_k[a], state_nsa_win_v[a],
                w_main, w_gate, nsa_w_o[a].astype(BF16), cmpk, cmpv, g1, b1, nsa_tabs_p, nsa_tabs_s)
            nsa_p.append(st_p)
            nsa_s.append(st_s)
        else:
            w_pad, b_pad = _swa_padded(swa_w_in[a], swa_b_in[a].reshape(1, -1))
            yp, ypb, ys, ysb, st_p, st_s = _swa_layer(
                yp, ypb, ys, ysb, state_swa_k[a], state_swa_v[a], swa_w_in[a].astype(BF16), swa_b_in[a].reshape(1, -1),
                w_pad.astype(BF16), b_pad, swa_sinks[a], swa_w_o[a].astype(BF16), g1, b1,
                swa_tabs_p, swa_tabs_pk, swa_tabs_s)
            swa_p.append(st_p)
            swa_s.append(st_s)
        yp, ypb, ys, ysb, cp, cs = _ffn_layer(i, yp, ypb, ys, ysb, state_conv[i], w_up_bf, ffn_conv_w[i],
                                              ffn_conv_b[i].reshape(1, -1), w_down_bf, g2, b2)
        conv_p.append(cp)
        conv_s.append(cs)

    p_cmp_k, p_cmp_v, p_slc_k, p_slc_v, p_nwin_k, p_nwin_v = [jnp.stack(t) for t in zip(*nsa_p)]
    s_cmp_k, s_cmp_v, s_slc_k, s_slc_v, s_nwin_k, s_nwin_v = [jnp.stack(t) for t in zip(*nsa_s)]
    p_swa_k, p_swa_v = [jnp.stack(t) for t in zip(*swa_p)]
    s_swa_k, s_swa_v = [jnp.stack(t) for t in zip(*swa_s)]
    return (yp.reshape(BATCH, SEQ, D_MODEL), ys.reshape(DEC_BATCH, 1, D_MODEL),
            p_cmp_k, p_cmp_v, p_slc_k, p_slc_v, p_nwin_k, p_nwin_v, p_swa_k, p_swa_v, jnp.stack(conv_p),
            s_cmp_k, s_cmp_v, s_slc_k, s_slc_v, s_nwin_k, s_nwin_v, s_swa_k, s_swa_v, jnp.stack(conv_s))
```
